```python
import jax, jax.numpy as jnp
from jax import lax
import numpy as np

D_MODEL = 1024
BATCH = 16
SEQ = 4096
DEPTH = 1
DEC_BATCH = 32
DEC_SEQ = 64
PAST_LEN = 4096

CHUNK = 64
D_POOL = 512
POOL_WINDOWS = (2, 4, 8, 16)
N_POOL_GROUPS = 4
POOL_GROUP = D_POOL // N_POOL_GROUPS
POOL_HIST = 15
N_HEADS = 4
HEAD_DK = 256
HEAD_DV = 256
D_QK = N_HEADS * HEAD_DK
D_V = N_HEADS * HEAD_DV
N_EXPERTS = 32
TOP_K = 4
D_FF = 1024
SWIGLU_ALPHA = 1.702
SWIGLU_LIMIT = 7.0
MOE_BLOCK = 256
LN_EPS = 1e-5
DEEPNORM_ALPHA = (2.0 * DEPTH) ** 0.25
DEEPNORM_BETA = (8.0 * DEPTH) ** -0.25
IN_SPLITS = (D_POOL, D_QK, D_QK, D_V, D_V, N_HEADS, N_HEADS, D_MODEL, D_MODEL)
D_IN = sum(IN_SPLITS)

kernel_name = "hybrid_pool_mlstm_moe_stream_step"


def _layer_norm(x, g, b):
    xf = x.astype(jnp.float32)
    mu = jnp.mean(xf, axis=-1, keepdims=True)
    var = jnp.mean(jnp.square(xf - mu), axis=-1, keepdims=True)
    y = (xf - mu) * lax.rsqrt(var + LN_EPS)
    if g is not None:
        y = y * g.astype(jnp.float32) + b.astype(jnp.float32)
    return y.astype(x.dtype)


def _pool_mixer(u, hist, pos0, w_group, pool_scale):
    B, L, _ = u.shape
    ext = jnp.concatenate([hist.astype(u.dtype), u], axis=1)
    csum = jnp.pad(jnp.cumsum(ext.astype(jnp.float32), axis=1), ((0, 0), (1, 0), (0, 0)))
    pos = pos0 + jnp.arange(L)
    s0 = POOL_HIST + 1
    outs = []
    for g, w in enumerate(POOL_WINDOWS):
        sl = slice(g * POOL_GROUP, (g + 1) * POOL_GROUP)
        win = csum[:, s0:s0 + L, sl] - csum[:, s0 - w:s0 - w + L, sl]
        cnt = jnp.minimum(w, pos + 1).astype(jnp.float32)[None, :, None]
        outs.append(win / cnt - u[:, :, sl].astype(jnp.float32))
    pooled = jnp.stack(outs, axis=2)
    mixed = jnp.einsum('blgc,gcd->blgd', pooled, w_group.astype(jnp.float32)).reshape(B, L, D_POOL)
    return (mixed * pool_scale.astype(jnp.float32)).astype(u.dtype), ext[:, -POOL_HIST:]


def _mlstm_chunk(carry, inp):
    C, n, m = carry
    q, k, v, ig, lf = inp
    L = q.shape[2]
    b = jnp.cumsum(lf, axis=-1)
    causal = jnp.tril(jnp.ones((L, L), dtype=bool))
    log_d = jnp.where(causal, b[..., :, None] - b[..., None, :] + ig[..., None, :], -jnp.inf)
    log_inter = b + m[..., None]
    m_t = jnp.maximum(log_inter, jnp.max(log_d, axis=-1))
    d = jnp.exp(log_d - m_t[..., None])
    s = jnp.einsum('bhtd,bhsd->bhts', q, k) * d
    inter = jnp.exp(log_inter - m_t)
    num = jnp.einsum('bhts,bhsv->bhtv', s, v) + inter[..., None] * jnp.einsum('bhtd,bhdv->bhtv', q, C)
    den = jnp.sum(s, axis=-1) + inter * jnp.einsum('bhtd,bhd->bht', q, n)
    h = num / jnp.maximum(jnp.abs(den), jnp.exp(-m_t))[..., None]
    b_last = b[..., -1]
    log_w = b_last[..., None] - b + ig
    m_new = jnp.maximum(b_last + m, jnp.max(log_w, axis=-1))
    decay = jnp.exp(b_last + m - m_new)
    wk = k * jnp.exp(log_w - m_new[..., None])[..., None]
    C_new = decay[..., None, None] * C + jnp.einsum('bhsd,bhsv->bhdv', wk, v)
    n_new = decay[..., None] * n + jnp.sum(wk, axis=2)
    return (C_new, n_new, m_new), h


def _mlstm_seq(q, k, v, ig, lf, C0, n0, m0):
    B, L = q.shape[0], q.shape[1]
    cl = min(CHUNK, L)
    nc = L // cl

    def to_chunks(a):
        return jnp.moveaxis(a.reshape((B, nc, cl) + a.shape[2:]), (1, 3), (0, 2))

    carry0 = (C0.astype(jnp.float32), n0.astype(jnp.float32), m0.astype(jnp.float32))
    (C, n, m), hs = lax.scan(_mlstm_chunk, carry0,
                             (to_chunks(q), to_chunks(k), to_chunks(v), to_chunks(ig), to_chunks(lf)))
    hs = jnp.moveaxis(hs, (0, 2), (1, 3)).reshape(B, L, N_HEADS, HEAD_DV)
    return hs, C, n, m


def _mixer_block(h, pool_hist, C0, n0, m0, pos0, w_in, b_if, w_group, pool_scale, mh_norm_g,
                 w_proj_a, w_proj_b, w_out):
    B, L, _ = h.shape
    f32 = jnp.float32
    proj = h @ w_in
    u, q, k, v, o, ig, fg, ga, gb = jnp.split(proj, np.cumsum(IN_SPLITS)[:-1].tolist(), axis=-1)
    a, pool_state = _pool_mixer(u, pool_hist, pos0, w_group, pool_scale)
    qh = q.reshape(B, L, N_HEADS, HEAD_DK).astype(f32)
    kh = k.reshape(B, L, N_HEADS, HEAD_DK).astype(f32) * (HEAD_DK ** -0.5)
    vh = v.reshape(B, L, N_HEADS, HEAD_DV).astype(f32)
    b_if = b_if.astype(f32)
    ig = ig.astype(f32) + b_if[:N_HEADS]
    lf = jax.nn.log_sigmoid(fg.astype(f32) + b_if[N_HEADS:])
    hm, C, n, m = _mlstm_seq(qh, kh, vh, ig, lf, C0, n0, m0)
    mu = jnp.mean(hm, axis=-1, keepdims=True)
    var = jnp.mean(jnp.square(hm - mu), axis=-1, keepdims=True)
    hm = ((hm - mu) * lax.rsqrt(var + LN_EPS)).reshape(B, L, D_V) * mh_norm_g.astype(f32)
    hm = (hm * jax.nn.sigmoid(o.astype(f32))).astype(h.dtype)
    merged = jax.nn.sigmoid(ga) * (a @ w_proj_a) + jax.nn.sigmoid(gb) * (hm @ w_proj_b)
    return (merged @ w_out, pool_state,
            C.astype(C0.dtype), n.astype(n0.dtype), m.astype(m0.dtype))


def _moe_block(h, w_router, b_router, w_gu, b_gu, w_down, b_down):
    B, L, D = h.shape
    f32 = jnp.float32
    N = B * L
    A = N * TOP_K
    xt = h.reshape(N, D)
    logits = (xt @ w_router).astype(f32) + b_router.astype(f32)
    top_v, top_i = lax.top_k(logits, TOP_K)
    gates = jax.nn.softmax(top_v, axis=-1)
    e_flat = top_i.reshape(A)
    tok = jnp.repeat(jnp.arange(N, dtype=jnp.int32), TOP_K)
    g_flat = gates.reshape(A)
    order = jnp.argsort(e_flat)
    e_sorted = e_flat[order]
    counts = jnp.bincount(e_flat, length=N_EXPERTS)
    padded = (counts + MOE_BLOCK - 1) // MOE_BLOCK * MOE_BLOCK
    pad_end = jnp.cumsum(padded)
    pad_start = pad_end - padded
    grp_start = jnp.cumsum(counts) - counts
    dest = pad_start[e_sorted] + jnp.arange(A) - grp_start[e_sorted]
    n_blocks = -(-A // MOE_BLOCK) + N_EXPERTS
    P = n_blocks * MOE_BLOCK
    row_tok = jnp.full((P,), N, dtype=jnp.int32).at[dest].set(tok[order])
    row_gate = jnp.zeros((P,), f32).at[dest].set(g_flat[order])
    block_e = jnp.minimum(jnp.searchsorted(pad_end, jnp.arange(n_blocks) * MOE_BLOCK, side='right'),
                          N_EXPERTS - 1)
    x_pad = jnp.concatenate([xt, jnp.zeros((1, D), xt.dtype)], axis=0)

    def body(acc, blk):
        rt, rg, e = blk
        xb = x_pad[rt]
        gu = (xb @ w_gu[e] + b_gu[e]).astype(f32)
        x_glu, x_lin = jnp.split(gu, 2, axis=-1)
        x_glu = jnp.minimum(x_glu, SWIGLU_LIMIT)
        x_lin = jnp.clip(x_lin, -SWIGLU_LIMIT, SWIGLU_LIMIT)
        act = x_glu * jax.nn.sigmoid(SWIGLU_ALPHA * x_glu) * (x_lin + 1.0)
        y = (act.astype(xt.dtype) @ w_down[e] + b_down[e]).astype(f32)
        return acc.at[rt].add(y * rg[:, None]), None

    acc, _ = lax.scan(body, jnp.zeros((N + 1, D), f32),
                      (row_tok.reshape(n_blocks, MOE_BLOCK), row_gate.reshape(n_blocks, MOE_BLOCK), block_e))
    return acc[:N].reshape(B, L, D).astype(h.dtype)


def _trunk_layer(x, c, pool_hist, C0, n0, m0, pos0, w_ada, b_ada, w_in, b_if, w_group, pool_scale,
                 mh_norm_g, w_proj_a, w_proj_b, w_out, ln1_g, ln1_b, w_router, b_router,
                 w_gu, b_gu, w_down, b_down, ln2_g, ln2_b):
    mod = jax.nn.silu(c) @ w_ada + b_ada
    sh1, sc1, g1, sh2, sc2, g2 = [t[:, None, :] for t in jnp.split(mod, 6, axis=-1)]
    h1 = _layer_norm(x, None, None) * (1 + sc1) + sh1
    mix, pool_state, C, n, m = _mixer_block(h1, pool_hist, C0, n0, m0, pos0, w_in, b_if, w_group,
                                            pool_scale, mh_norm_g, w_proj_a, w_proj_b, w_out)
    x1 = _layer_norm(DEEPNORM_ALPHA * x + g1 * mix, ln1_g, ln1_b)
    h2 = _layer_norm(x1, None, None) * (1 + sc2) + sh2
    ffn = _moe_block(h2, w_router, b_router, w_gu, b_gu, w_down, b_down)
    x2 = _layer_norm(DEEPNORM_ALPHA * x1 + g2 * ffn, ln2_g, ln2_b)
    return x2, pool_state, C, n, m


def setup_inputs(seed: int = 0) -> dict:
    key = jax.random.key(seed)
    ks = jax.random.split(key, 32)
    nrm = lambda k, s: jax.random.normal(k, s, jnp.float32)
    Dl = DEPTH
    b_if = jnp.concatenate([0.1 * nrm(ks[0], (Dl, N_HEADS)),
                            jnp.broadcast_to(jnp.linspace(3.0, 6.0, N_HEADS, dtype=jnp.float32), (Dl, N_HEADS))
                            + 0.1 * nrm(ks[1], (Dl, N_HEADS))], axis=-1)
    return {
        "x_prompt": nrm(ks[2], (BATCH, SEQ, D_MODEL)),
        "x_sample": nrm(ks[3], (DEC_BATCH, DEC_SEQ, D_MODEL)),
        "state_pool": nrm(ks[4], (Dl, DEC_BATCH, POOL_HIST, D_POOL)),
        "state_mlstm_C": 0.1 * nrm(ks[5], (Dl, DEC_BATCH, N_HEADS, HEAD_DK, HEAD_DV)),
        "state_mlstm_n": 0.1 * nrm(ks[6], (Dl, DEC_BATCH, N_HEADS, HEAD_DK)),
        "state_mlstm_m": 0.5 * nrm(ks[7], (Dl, DEC_BATCH, N_HEADS)),
        "c_prompt": nrm(ks[8], (BATCH, D_MODEL)),
        "c_sample": nrm(ks[9], (DEC_BATCH, D_MODEL)),
        "w_ada": nrm(ks[10], (Dl, D_MODEL, 6 * D_MODEL)) * D_MODEL ** -0.5,
        "b_ada": 0.02 * nrm(ks[11], (Dl, 6 * D_MODEL)),
        "w_in": nrm(ks[12], (Dl, D_MODEL, D_IN)) * D_MODEL ** -0.5,
        "b_if": b_if,
        "w_group": nrm(ks[13], (Dl, N_POOL_GROUPS, POOL_GROUP, POOL_GROUP)) * POOL_GROUP ** -0.5,
        "pool_scale": 1.0 + 0.1 * nrm(ks[14], (Dl, D_POOL)),
        "mh_norm_g": 1.0 + 0.1 * nrm(ks[15], (Dl, D_V)),
        "w_proj_a": nrm(ks[16], (Dl, D_POOL, D_MODEL)) * D_POOL ** -0.5,
        "w_proj_b": nrm(ks[17], (Dl, D_V, D_MODEL)) * D_V ** -0.5,
        "w_out": nrm(ks[18], (Dl, D_MODEL, D_MODEL)) * (D_MODEL ** -0.5 * DEEPNORM_BETA),
        "ln1_g": 1.0 + 0.1 * nrm(ks[19], (Dl, D_MODEL)),
        "ln1_b": 0.02 * nrm(ks[20], (Dl, D_MODEL)),
        "w_router": nrm(ks[21], (Dl, D_MODEL, N_EXPERTS)) * D_MODEL ** -0.5,
        "b_router": 0.01 * nrm(ks[22], (Dl, N_EXPERTS)),
        "w_gu": nrm(ks[23], (Dl, N_EXPERTS, D_MODEL, 2 * D_FF)) * D_MODEL ** -0.5,
        "b_gu": 0.02 * nrm(ks[24], (Dl, N_EXPERTS, 2 * D_FF)),
        "w_down": nrm(ks[25], (Dl, N_EXPERTS, D_FF, D_MODEL)) * (D_FF ** -0.5 * DEEPNORM_BETA),
        "b_down": 0.02 * nrm(ks[26], (Dl, N_EXPERTS, D_MODEL)),
        "ln2_g": 1.0 + 0.1 * nrm(ks[27], (Dl, D_MODEL)),
        "ln2_b": 0.02 * nrm(ks[28], (Dl, D_MODEL)),
    }


def reference(x_prompt, x_sample, state_pool, state_mlstm_C, state_mlstm_n, state_mlstm_m, c_prompt, c_sample,
              w_ada, b_ada, w_in, b_if, w_group, pool_scale, mh_norm_g, w_proj_a, w_proj_b, w_out,
              ln1_g, ln1_b, w_router, b_router, w_gu, b_gu, w_down, b_down, ln2_g, ln2_b):
    yp, ys = x_prompt, x_sample
    bp = x_prompt.shape[0]
    pool_p, C_p, n_p, m_p = [], [], [], []
    pool_s, C_s, n_s, m_s = [], [], [], []
    for l in range(DEPTH):
        lw = (w_ada[l], b_ada[l], w_in[l], b_if[l], w_group[l], pool_scale[l], mh_norm_g[l],
              w_proj_a[l], w_proj_b[l], w_out[l], ln1_g[l], ln1_b[l], w_router[l], b_router[l],
              w_gu[l], b_gu[l], w_down[l], b_down[l], ln2_g[l], ln2_b[l])
        dt = x_prompt.dtype
        yp, ps, Cp, np_, mp = _trunk_layer(
            yp, c_prompt,
            jnp.zeros((bp, POOL_HIST, D_POOL), dt), jnp.zeros((bp, N_HEADS, HEAD_DK, HEAD_DV), dt),
            jnp.zeros((bp, N_HEADS, HEAD_DK), dt), jnp.zeros((bp, N_HEADS), dt), 0, *lw)
        ys, ss, Cs, ns, ms = _trunk_layer(
            ys, c_sample, state_pool[l], state_mlstm_C[l], state_mlstm_n[l], state_mlstm_m[l], PAST_LEN, *lw)
        pool_p.append(ps); C_p.append(Cp); n_p.append(np_); m_p.append(mp)
        pool_s.append(ss); C_s.append(Cs); n_s.append(ns); m_s.append(ms)
    return (yp, ys, jnp.stack(pool_p), jnp.stack(C_p), jnp.stack(n_p), jnp.stack(m_p),
            jnp.stack(pool_s), jnp.stack(C_s), jnp.stack(n_s), jnp.stack(m_s))
```

```python
import functools

import jax
import jax.numpy as jnp
from jax import lax
from jax.experimental import pallas as pl
from jax.experimental.pallas import tpu as pltpu

D_MODEL = 1024
DEPTH = 1
D_POOL = 512
POOL_WINDOWS = (2, 4, 8, 16)
POOL_GROUP = 128
POOL_HIST = 15
N_HEADS = 4
HEAD_DK = 256
HEAD_DV = 256
N_EXPERTS = 32
TOP_K = 4
D_FF = 1024
SWIGLU_ALPHA = 1.702
SWIGLU_LIMIT = 7.0
LN_EPS = 1e-5
DEEPNORM_ALPHA = (2.0 * DEPTH) ** 0.25
PAST_LEN = 4096

LANES = 128
SUBLANES = 8
HIST_ROWS = POOL_HIST + 1
MOE_ROWS = 256
RANK_TILE = 512
ROW_TILE = 256
VMEM_LIMIT = 56 * 1024 * 1024
NEG_BIG = -1e30

F32 = jnp.float32
BF16 = jnp.bfloat16


def _dot(a, b):
    return jnp.dot(a, b, preferred_element_type=F32)


def _ln(x):
    mu = jnp.mean(x, axis=-1, keepdims=True)
    xc = x - mu
    var = jnp.mean(xc * xc, axis=-1, keepdims=True)
    return xc * lax.rsqrt(var + LN_EPS)


def _sigmoid(x):
    return 1.0 / (1.0 + jnp.exp(-x))


def _log_sigmoid(x):
    return jnp.minimum(x, 0.0) - jnp.log(1.0 + jnp.exp(-jnp.abs(x)))


def _split3(x):
    hi = x.astype(BF16)
    r1 = x - hi.astype(F32)
    mid = r1.astype(BF16)
    lo = (r1 - mid.astype(F32)).astype(BF16)
    return hi, mid, lo


def _const_spec(shape):
    nd = len(shape)
    return pl.BlockSpec(shape, lambda *_: (0,) * nd, pipeline_mode=pl.Buffered(1))


def _ada_kernel(c_ref, w_ref, b_ref, o_ref):
    c = c_ref[...]
    s = c * _sigmoid(c)
    o_ref[...] = _dot(s.astype(BF16), w_ref[...].astype(BF16)) + b_ref[...]


def _ada(c, w_ada, b_ada):
    nb, d = c.shape
    n_out = w_ada.shape[1]
    blk = D_MODEL
    return pl.pallas_call(
        _ada_kernel,
        grid=(n_out // blk,),
        in_specs=[
            pl.BlockSpec((nb, d), lambda i: (0, 0)),
            pl.BlockSpec((d, blk), lambda i: (0, i)),
            pl.BlockSpec((1, blk), lambda i: (0, i)),
        ],
        out_specs=pl.BlockSpec((nb, blk), lambda i: (0, i)),
        out_shape=jax.ShapeDtypeStruct((nb, n_out), F32),
        name="ada_mod",
    )(c, w_ada, b_ada.reshape(1, n_out))


def _mixer_kernel(x_ref, mod_ref, hist0_ref, c0_ref, n0_ref, m0_ref,
                  wu_ref, wq_ref, wk_ref, wv_ref, wo_ref, wga_ref, wgb_ref, wif_ref, bif_ref,
                  wgrp_ref, pscale_ref, mhg_ref, wpa_ref, wpb_ref, wout_ref, ln1g_ref, ln1b_ref,
                  wr_ref, br_ref,
                  x1_ref, h2_ref, eidx_ref, gate_ref, hist_ref, c_ref, n_ref, m_ref,
                  c_s, n_s, m_s, pbuf, *, T, pos0):
    j = pl.program_id(1)
    nj = pl.num_programs(1)

    @pl.when(j == 0)
    def _():
        c_s[...] = c0_ref[0]
        n_s[...] = n0_ref[0]
        m_s[...] = m0_ref[0]
        pbuf[0:1, :] = jnp.zeros((1, D_POOL), F32)
        pbuf[1:HIST_ROWS, :] = hist0_ref[0]

    x = x_ref[0]
    mod = mod_ref[0]
    sh1, sc1, g1, sh2, sc2, g2 = [mod[i:i + 1] for i in range(6)]
    h1b = (_ln(x) * (1.0 + sc1) + sh1).astype(BF16)

    u = _dot(h1b, wu_ref[...])
    pbuf[HIST_ROWS:HIST_ROWS + T, :] = u
    pos = pos0 + j * T + lax.broadcasted_iota(jnp.int32, (T, 1), 0)
    mixed = []
    for g, w in enumerate(POOL_WINDOWS):
        cols = slice(g * POOL_GROUP, (g + 1) * POOL_GROUP)
        win = pbuf[HIST_ROWS:HIST_ROWS + T, cols]
        for i in range(1, w):
            win = win + pbuf[HIST_ROWS - i:HIST_ROWS - i + T, cols]
        cnt = jnp.minimum(w, pos + 1).astype(F32)
        pooled = win / cnt - u[:, cols]
        mixed.append(_dot(pooled.astype(BF16), wgrp_ref[g]))
    a = jnp.concatenate(mixed, axis=1) * pscale_ref[...]
    pbuf[0:HIST_ROWS, :] = pbuf[T:T + HIST_ROWS, :]

    gif = _dot(h1b, wif_ref[...]) + bif_ref[...]
    gif_t = gif.T
    lf_c = _log_sigmoid(gif)
    lf_r = _log_sigmoid(gif_t)
    row = lax.broadcasted_iota(jnp.int32, (T, T), 0)
    col = lax.broadcasted_iota(jnp.int32, (T, T), 1)
    causal = col <= row
    tri = causal.astype(BF16)
    tri_t = (row <= col).astype(BF16)
    b_c = sum(_dot(tri, p) for p in _split3(lf_c))
    b_r = sum(_dot(p, tri_t) for p in _split3(lf_r))

    q = _dot(h1b, wq_ref[...])
    k = _dot(h1b, wk_ref[...]) * (HEAD_DK ** -0.5)
    v = _dot(h1b, wv_ref[...])
    m_all = m_s[...]
    lane = lax.broadcasted_iota(jnp.int32, (1, LANES), 1)
    m_next = m_all
    heads = []
    for h in range(N_HEADS):
        sl = slice(h * HEAD_DK, (h + 1) * HEAD_DK)
        qh, kh = q[:, sl], k[:, sl]
        qb, kb, vb = qh.astype(BF16), kh.astype(BF16), v[:, sl].astype(BF16)
        bc = b_c[:, N_HEADS + h:N_HEADS + h + 1]
        br = b_r[N_HEADS + h:N_HEADS + h + 1, :]
        ig_c = gif[:, h:h + 1]
        ig_r = gif_t[h:h + 1, :]
        m_prev = m_all[:, h:h + 1]
        log_d = jnp.where(causal, bc - br + ig_r, -jnp.inf)
        log_inter = bc + m_prev
        m_t = jnp.maximum(log_inter, jnp.max(log_d, axis=1, keepdims=True))
        d = jnp.exp(log_d - m_t)
        s = lax.dot_general(qb, kb, (((1,), (1,)), ((), ())), preferred_element_type=F32) * d
        inter = jnp.exp(log_inter - m_t)
        c_h = c_s[h]
        n_h = n_s[h:h + 1, :]
        num = _dot(s.astype(BF16), vb) + inter * _dot(qb, c_h.astype(BF16))
        den = jnp.sum(s, axis=1, keepdims=True) + inter * jnp.sum(qh * n_h, axis=1, keepdims=True)
        heads.append(_ln(num / jnp.maximum(jnp.abs(den), jnp.exp(-m_t))))
        b_last = bc[T - 1:T, :]
        lw_c = b_last - bc + ig_c
        lw_r = b_last - br + ig_r
        m_new = jnp.maximum(b_last + m_prev, jnp.max(lw_r, axis=1, keepdims=True))
        decay = jnp.exp(b_last + m_prev - m_new)
        wk = kh * jnp.exp(lw_c - m_new)
        c_s[h] = decay * c_h + _dot(wk.T.astype(BF16), vb)
        n_s[h:h + 1, :] = decay * n_h + jnp.sum(wk, axis=0, keepdims=True)
        m_next = jnp.where(lane == h, m_new, m_next)
    m_s[...] = m_next

    hm = jnp.concatenate(heads, axis=1) * mhg_ref[...]
    hm = hm * _sigmoid(_dot(h1b, wo_ref[...]))
    merged = (_sigmoid(_dot(h1b, wga_ref[...])) * _dot(a.astype(BF16), wpa_ref[...])
              + _sigmoid(_dot(h1b, wgb_ref[...])) * _dot(hm.astype(BF16), wpb_ref[...]))
    mix = _dot(merged.astype(BF16), wout_ref[...])
    x1 = _ln(DEEPNORM_ALPHA * x + g1 * mix) * ln1g_ref[...] + ln1b_ref[...]
    h2 = _ln(x1) * (1.0 + sc2) + sh2
    x1_ref[0] = x1
    h2_ref[0] = h2

    logits = _dot(h2.astype(BF16), wr_ref[...]) + br_ref[...]
    lt = logits.T[0:N_EXPERTS, :]
    eio = lax.broadcasted_iota(jnp.int32, (N_EXPERTS, T), 0)
    vals, idxs = [], []
    for _ in range(TOP_K):
        mx = jnp.max(lt, axis=0, keepdims=True)
        idx = jnp.min(jnp.where(lt == mx, eio, N_EXPERTS), axis=0, keepdims=True)
        vals.append(mx)
        idxs.append(idx)
        lt = jnp.where(eio == idx, -jnp.inf, lt)
    ex = [jnp.exp(vk - vals[0]) for vk in vals]
    tot = ex[0] + ex[1] + ex[2] + ex[3]
    eidx_ref[0] = jnp.concatenate(idxs, axis=0)
    gate_ref[0] = jnp.concatenate([e / tot for e in ex], axis=0)

    @pl.when(j == nj - 1)
    def _():
        hist_ref[0] = pbuf[1:HIST_ROWS, :]
        c_ref[0] = c_s[...]
        n_ref[0] = n_s[...]
        m_ref[0] = m_s[...]


def _mixer(x, mod, hist0, c0, n0, m0, wts, *, T, pos0):
    B, L, D = x.shape
    nj = L // T
    N = B * L
    kern = functools.partial(_mixer_kernel, T=T, pos0=pos0)
    seq_map = lambda b, j: (b, j, 0)
    st3 = lambda b, j: (b, 0, 0)
    st4 = lambda b, j: (b, 0, 0, 0)
    in_specs = [
        pl.BlockSpec((1, T, D), seq_map),
        pl.BlockSpec((1, 6, D), st3),
        pl.BlockSpec((1, POOL_HIST, D_POOL), st3),
        pl.BlockSpec((1, N_HEADS, HEAD_DK, HEAD_DV), st4),
        pl.BlockSpec((1, N_HEADS, HEAD_DK), st3),
        pl.BlockSpec((1, 1, LANES), st3),
    ] + [_const_spec(w.shape) for w in wts]
    out_specs = [
        pl.BlockSpec((1, T, D), seq_map),
        pl.BlockSpec((1, T, D), seq_map),
        pl.BlockSpec((1, TOP_K, T), lambda b, j: (b * nj + j, 0, 0)),
        pl.BlockSpec((1, TOP_K, T), lambda b, j: (b * nj + j, 0, 0)),
        pl.BlockSpec((1, POOL_HIST, D_POOL), st3),
        pl.BlockSpec((1, N_HEADS, HEAD_DK, HEAD_DV), st4),
        pl.BlockSpec((1, N_HEADS, HEAD_DK), st3),
        pl.BlockSpec((1, 1, LANES), st3),
    ]
    out_shape = [
        jax.ShapeDtypeStruct((B, L, D), F32),
        jax.ShapeDtypeStruct((B, L, D), F32),
        jax.ShapeDtypeStruct((N // T, TOP_K, T), jnp.int32),
        jax.ShapeDtypeStruct((N // T, TOP_K, T), F32),
        jax.ShapeDtypeStruct((B, POOL_HIST, D_POOL), F32),
        jax.ShapeDtypeStruct((B, N_HEADS, HEAD_DK, HEAD_DV), F32),
        jax.ShapeDtypeStruct((B, N_HEADS, HEAD_DK), F32),
        jax.ShapeDtypeStruct((B, 1, LANES), F32),
    ]
    scratch = [
        pltpu.VMEM((N_HEADS, HEAD_DK, HEAD_DV), F32),
        pltpu.VMEM((N_HEADS, HEAD_DK), F32),
        pltpu.VMEM((1, LANES), F32),
        pltpu.VMEM((T + HIST_ROWS, D_POOL), F32),
    ]
    return pl.pallas_call(
        kern,
        grid=(B, nj),
        in_specs=in_specs,
        out_specs=out_specs,
        out_shape=out_shape,
        scratch_shapes=scratch,
        compiler_params=pltpu.CompilerParams(
            dimension_semantics=("arbitrary", "arbitrary"), vmem_limit_bytes=VMEM_LIMIT),
        name="mixer",
    )(x, mod, hist0, c0, n0, m0, *wts)


def _rank_kernel(e_ref, rank_ref, cnt_ref, base_s, *, T):
    @pl.when(pl.program_id(0) == 0)
    def _():
        base_s[...] = jnp.zeros_like(base_s)

    e = e_ref[...]
    eio = lax.broadcasted_iota(jnp.int32, (N_EXPERTS, T), 0)
    hits = [eio == e[k:k + 1, :] for k in range(TOP_K)]
    oh = sum(h.astype(F32) for h in hits)
    row = lax.broadcasted_iota(jnp.int32, (T, T), 0)
    col = lax.broadcasted_iota(jnp.int32, (T, T), 1)
    before = _dot(oh.astype(BF16), (row < col).astype(BF16))
    tot = before + base_s[:, 0:1]
    ranks = [jnp.sum(jnp.where(h, tot, 0.0), axis=0, keepdims=True) for h in hits]
    rank_ref[...] = jnp.concatenate(ranks, axis=0).astype(jnp.int32)
    base_s[...] = base_s[...] + jnp.sum(oh, axis=1, keepdims=True)
    cnt_ref[...] = base_s[...].astype(jnp.int32)


def _ranks(eidx):
    K, N = eidx.shape
    T = RANK_TILE
    return pl.pallas_call(
        functools.partial(_rank_kernel, T=T),
        grid=(N // T,),
        in_specs=[pl.BlockSpec((K, T), lambda i: (0, i))],
        out_specs=[pl.BlockSpec((K, T), lambda i: (0, i)),
                   pl.BlockSpec((N_EXPERTS, LANES), lambda i: (0, 0))],
        out_shape=[jax.ShapeDtypeStruct((K, N), jnp.int32),
                   jax.ShapeDtypeStruct((N_EXPERTS, LANES), jnp.int32)],
        scratch_shapes=[pltpu.VMEM((N_EXPERTS, LANES), F32)],
        compiler_params=pltpu.CompilerParams(dimension_semantics=("arbitrary",)),
        name="expert_rank",
    )(eidx)


def _dispatch_kernel(fill_ref, dest_ref, h_ref, xs_ref, zbuf, sem, *, T):
    def row_copy(t, d):
        return pltpu.make_async_copy(h_ref.at[pl.ds(t, 1)], xs_ref.at[pl.ds(d, 1)], sem)

    def zero_fill(finish):
        def go(off, size):
            cp = pltpu.make_async_copy(zbuf.at[pl.ds(0, size)], xs_ref.at[pl.ds(off, size)], sem)
            cp.wait() if finish else cp.start()

        for e in range(N_EXPERTS):
            off, pad = fill_ref[e], fill_ref[N_EXPERTS + e]
            head = pad & (SUBLANES - 1)
            for r in range(SUBLANES - 1):
                pl.when(r < head)(functools.partial(go, off + r, 1))
            off = pl.multiple_of(off + head, SUBLANES)
            for bit in range(SUBLANES.bit_length() - 1, MOE_ROWS.bit_length() - 1):
                size = 1 << bit
                pl.when((pad & size) != 0)(functools.partial(go, off, size))
                off = pl.multiple_of(off + (pad & size), SUBLANES)

        def tail(blk, c):
            go(pl.multiple_of(blk * MOE_ROWS, MOE_ROWS), MOE_ROWS)
            return c
        lax.fori_loop(fill_ref[2 * N_EXPERTS], xs_ref.shape[0] // MOE_ROWS, tail, 0)

    @pl.when(pl.program_id(0) == 0)
    def _():
        zbuf[...] = jnp.zeros_like(zbuf)
        zero_fill(False)
        zero_fill(True)

    def body(t, c):
        for k in range(TOP_K):
            row_copy(t, dest_ref[k, t]).start()
        return c

    lax.fori_loop(0, T, body, 0, unroll=8)
    for _ in range(TOP_K):
        pltpu.make_async_copy(h_ref, xs_ref.at[pl.ds(0, T)], sem).wait()


def _dispatch(fill_start, dest, h2, n_rows, *, T):
    N, D = h2.shape
    grid_spec = pltpu.PrefetchScalarGridSpec(
        num_scalar_prefetch=1,
        grid=(N // T,),
        in_specs=[pl.BlockSpec((TOP_K, T), lambda i, fill: (0, i), memory_space=pltpu.SMEM),
                  pl.BlockSpec((T, D), lambda i, fill: (i, 0))],
        out_specs=pl.BlockSpec(memory_space=pl.ANY),
        scratch_shapes=[pltpu.VMEM((MOE_ROWS, D), h2.dtype), pltpu.SemaphoreType.DMA(())],
    )
    return pl.pallas_call(
        functools.partial(_dispatch_kernel, T=T),
        grid_spec=grid_spec,
        out_shape=jax.ShapeDtypeStruct((n_rows, D), h2.dtype),
        compiler_params=pltpu.CompilerParams(dimension_semantics=("arbitrary",)),
        name="dispatch",
    )(fill_start, dest, h2)


def _moe_kernel(be_ref, nu_ref, xs_ref, wgu_ref, bgu_ref, wd_ref, bd_ref, y_ref):
    i = pl.program_id(0)

    @pl.when(i < nu_ref[0])
    def _():
        gu = _dot(xs_ref[...].astype(BF16), wgu_ref[0]) + bgu_ref[0]
        glu = jnp.minimum(gu[:, :D_FF], SWIGLU_LIMIT)
        lin = jnp.clip(gu[:, D_FF:], -SWIGLU_LIMIT, SWIGLU_LIMIT)
        act = glu * _sigmoid(SWIGLU_ALPHA * glu) * (lin + 1.0)
        y_ref[...] = _dot(act.astype(BF16), wd_ref[0]) + bd_ref[0]

    @pl.when(i >= nu_ref[0])
    def _():
        y_ref[...] = jnp.zeros_like(y_ref)


def _moe(block_e, n_used, xs, w_gu, b_gu, w_down, b_down):
    P, D = xs.shape
    R = MOE_ROWS
    nb = P // R
    xs_map = lambda i, be, nu: (jnp.minimum(i, nu[0] - 1), 0)
    w_map = lambda i, be, nu: (be[i], 0, 0)
    grid_spec = pltpu.PrefetchScalarGridSpec(
        num_scalar_prefetch=2,
        grid=(nb,),
        in_specs=[
            pl.BlockSpec((R, D), xs_map),
            pl.BlockSpec((1, D, 2 * D_FF), w_map),
            pl.BlockSpec((1, 1, 2 * D_FF), w_map),
            pl.BlockSpec((1, D_FF, D), w_map),
            pl.BlockSpec((1, 1, D), w_map),
        ],
        out_specs=pl.BlockSpec((R, D), lambda i, be, nu: (i, 0)),
    )
    return pl.pallas_call(
        _moe_kernel,
        grid_spec=grid_spec,
        out_shape=jax.ShapeDtypeStruct((P, D), F32),
        compiler_params=pltpu.CompilerParams(
            dimension_semantics=("arbitrary",), vmem_limit_bytes=VMEM_LIMIT),
        name="moe_experts",
    )(block_e, n_used, xs, w_gu, b_gu, w_down, b_down)


def _combine_kernel(dcur_ref, dnxt_ref, gate_ref, x1_ref, mod_ref, g_ref, b_ref, y_hbm,
                    x2_ref, ybuf, sem, *, T):
    nj = pl.num_programs(1)
    step = pl.program_id(0) * nj + pl.program_id(1)
    n_steps = pl.num_programs(0) * nj
    slot = step % 2

    def issue(dref, s):
        def body(t, c):
            for k in range(TOP_K):
                pltpu.make_async_copy(y_hbm.at[pl.ds(dref[k, t], 1)], ybuf.at[s, k, pl.ds(t, 1)],
                                      sem.at[s]).start()
            return c
        lax.fori_loop(0, T, body, 0, unroll=8)

    @pl.when(step == 0)
    def _():
        issue(dcur_ref, 0)

    @pl.when(step + 1 < n_steps)
    def _():
        issue(dnxt_ref, 1 - slot)

    for k in range(TOP_K):
        pltpu.make_async_copy(y_hbm.at[pl.ds(0, T)], ybuf.at[slot, k], sem.at[slot]).wait()

    gates = jnp.concatenate([gate_ref[...], jnp.zeros((LANES - TOP_K, T), F32)], axis=0).T
    ffn = gates[:, 0:1] * ybuf[slot, 0]
    for k in range(1, TOP_K):
        ffn = ffn + gates[:, k:k + 1] * ybuf[slot, k]
    bt, lt, d = x1_ref.shape
    g2 = mod_ref[...][:, 5:6, :]
    z = DEEPNORM_ALPHA * x1_ref[...] + g2 * ffn.reshape(bt, lt, d)
    x2_ref[...] = _ln(z) * g_ref[...] + b_ref[...]


def _combine(dest, gates, x1, mod, ln_g, ln_b, y, *, T):
    B, L, D = x1.shape
    bt, lt = (1, T) if L >= T else (T // L, L)
    nj = L // lt
    n_tiles = (B // bt) * nj
    cur = lambda b, j: (0, b * nj + j)
    nxt = lambda b, j: (0, jnp.minimum(b * nj + j + 1, n_tiles - 1))
    return pl.pallas_call(
        functools.partial(_combine_kernel, T=T),
        grid=(B // bt, nj),
        in_specs=[
            pl.BlockSpec((TOP_K, T), cur, memory_space=pltpu.SMEM),
            pl.BlockSpec((TOP_K, T), nxt, memory_space=pltpu.SMEM),
            pl.BlockSpec((TOP_K, T), cur),
            pl.BlockSpec((bt, lt, D), lambda b, j: (b, j, 0)),
            pl.BlockSpec((bt, 6, D), lambda b, j: (b, 0, 0)),
            pl.BlockSpec((1, D), lambda b, j: (0, 0)),
            pl.BlockSpec((1, D), lambda b, j: (0, 0)),
            pl.BlockSpec(memory_space=pl.ANY),
        ],
        out_specs=pl.BlockSpec((bt, lt, D), lambda b, j: (b, j, 0)),
        out_shape=jax.ShapeDtypeStruct((B, L, D), F32),
        scratch_shapes=[pltpu.VMEM((2, TOP_K, T, D), F32), pltpu.SemaphoreType.DMA((2,))],
        compiler_params=pltpu.CompilerParams(
            dimension_semantics=("arbitrary", "arbitrary"), vmem_limit_bytes=VMEM_LIMIT),
        name="combine",
    )(dest, dest, gates, x1, mod, ln_g, ln_b, y)


def _routing_tables(eidx, rank, counts, n_blocks):
    R = MOE_ROWS
    padded = (counts + R - 1) // R * R
    pad_end = jnp.cumsum(padded)
    pad_start = pad_end - padded
    experts = jnp.arange(N_EXPERTS, dtype=jnp.int32)
    start_of = jnp.sum(jnp.where(eidx[..., None] == experts, pad_start, 0), axis=-1)
    dest = (rank + start_of).astype(jnp.int32)
    blk0 = jnp.arange(n_blocks, dtype=jnp.int32) * R
    block_e = jnp.sum((blk0[:, None] >= pad_end[None, :]).astype(jnp.int32), axis=1)
    block_e = jnp.minimum(block_e, N_EXPERTS - 1)
    n_used = (pad_end[-1:] // R).astype(jnp.int32)
    fill = jnp.concatenate([pad_start + counts, padded - counts, n_used]).astype(jnp.int32)
    return dest, block_e, n_used, fill


def _group_forward(x, mod, hist0, c0, n0, m0, mixer_w, moe_w, ln2_g, ln2_b, *, T, pos0):
    B, L, D = x.shape
    N = B * L
    m0p = jnp.pad(m0, ((0, 0), (0, LANES - N_HEADS))).reshape(B, 1, LANES)
    x1, h2, eidx, gates, hist, c_new, n_new, m_new = _mixer(
        x, mod, hist0, c0, n0, m0p, mixer_w, T=T, pos0=pos0)
    eidx = eidx.transpose(1, 0, 2).reshape(TOP_K, N)
    gates = gates.transpose(1, 0, 2).reshape(TOP_K, N)
    rank, cnt = _ranks(eidx)
    n_blocks = -(-(N * TOP_K) // MOE_ROWS) + N_EXPERTS
    dest, block_e, n_used, fill_start = _routing_tables(eidx, rank, cnt[:, 0], n_blocks)
    xs = _dispatch(fill_start, dest, h2.reshape(N, D), n_blocks * MOE_ROWS, T=ROW_TILE)
    y = _moe(block_e, n_used, xs, *moe_w)
    x2 = _combine(dest, gates, x1, mod, ln2_g, ln2_b, y, T=ROW_TILE)
    return x2, hist, c_new, n_new, m_new.reshape(B, LANES)[:, :N_HEADS]


def kernel(x_prompt, x_sample, state_pool, state_mlstm_C, state_mlstm_n, state_mlstm_m, c_prompt, c_sample,
           w_ada, b_ada, w_in, b_if, w_group, pool_scale, mh_norm_g, w_proj_a, w_proj_b, w_out,
           ln1_g, ln1_b, w_router, b_router, w_gu, b_gu, w_down, b_down, ln2_g, ln2_b):
    assert w_ada.shape[0] == DEPTH == 1
    l = 0
    bp, bs = x_prompt.shape[0], x_sample.shape[0]
    dt = x_prompt.dtype
    row = lambda a: a.reshape(1, -1)

    wi = w_in[l]
    o0 = 0
    parts = []
    for width in (D_POOL, N_HEADS * HEAD_DK, N_HEADS * HEAD_DK, N_HEADS * HEAD_DV, N_HEADS * HEAD_DV):
        parts.append(wi[:, o0:o0 + width].astype(BF16))
        o0 += width
    w_if = jnp.pad(wi[:, o0:o0 + 2 * N_HEADS], ((0, 0), (0, LANES - 2 * N_HEADS))).astype(BF16)
    o0 += 2 * N_HEADS
    w_ga = wi[:, o0:o0 + D_MODEL].astype(BF16)
    w_gb = wi[:, o0 + D_MODEL:o0 + 2 * D_MODEL].astype(BF16)
    w_u, w_q, w_k, w_v, w_o = parts
    b_if_row = jnp.pad(b_if[l], (0, LANES - 2 * N_HEADS)).reshape(1, LANES)
    w_r = jnp.pad(w_router[l], ((0, 0), (0, LANES - N_EXPERTS))).astype(BF16)
    b_r = jnp.pad(b_router[l], (0, LANES - N_EXPERTS), constant_values=NEG_BIG).reshape(1, LANES)
    mixer_w = (w_u, w_q, w_k, w_v, w_o, w_ga, w_gb, w_if, b_if_row,
               w_group[l].astype(BF16), row(pool_scale[l]), row(mh_norm_g[l]),
               w_proj_a[l].astype(BF16), w_proj_b[l].astype(BF16), w_out[l].astype(BF16),
               row(ln1_g[l]), row(ln1_b[l]), w_r, b_r)
    moe_w = (w_gu[l].astype(BF16), b_gu[l].reshape(N_EXPERTS, 1, 2 * D_FF),
             w_down[l].astype(BF16), b_down[l].reshape(N_EXPERTS, 1, D_MODEL))

    mod = _ada(jnp.concatenate([c_prompt, c_sample], axis=0), w_ada[l], b_ada[l])
    mod = mod.reshape(bp + bs, 6, D_MODEL)

    zeros = lambda *s: jnp.zeros(s, dt)
    yp, pool_p, c_p, n_p, m_p = _group_forward(
        x_prompt, mod[:bp], zeros(bp, POOL_HIST, D_POOL), zeros(bp, N_HEADS, HEAD_DK, HEAD_DV),
        zeros(bp, N_HEADS, HEAD_DK), zeros(bp, N_HEADS), mixer_w, moe_w, row(ln2_g[l]), row(ln2_b[l]),
        T=min(256, x_prompt.shape[1]), pos0=0)
    ys, pool_s, c_s, n_s, m_s = _group_forward(
        x_sample, mod[bp:], state_pool[l], state_mlstm_C[l], state_mlstm_n[l], state_mlstm_m[l],
        mixer_w, moe_w, row(ln2_g[l]), row(ln2_b[l]), T=min(256, x_sample.shape[1]), pos0=PAST_LEN)
    st = lambda a: a[None]
    return (yp, ys, st(pool_p), st(c_p), st(n_p), st(m_p), st(pool_s), st(c_s), st(n_s), st(m_s))
```

```python
import functools

import jax
import jax.numpy as jnp
from jax import lax
from jax.experimental import pallas as pl
from jax.experimental.pallas import tpu as pltpu

D_MODEL = 1024
DEPTH = 1
D_POOL = 512
POOL_WINDOWS = (2, 4, 8, 16)
POOL_GROUP = 128
POOL_HIST = 15
N_HEADS = 4
HEAD_DK = 256
HEAD_DV = 256
N_EXPERTS = 32
TOP_K = 4
D_FF = 1024
SWIGLU_ALPHA = 1.702
SWIGLU_LIMIT = 7.0
LN_EPS = 1e-5
DEEPNORM_ALPHA = (2.0 * DEPTH) ** 0.25
PAST_LEN = 4096

LANES = 128
SUBLANES = 8
HIST_ROWS = POOL_HIST + 1
MOE_ROWS = 512
FF_CHUNK = 256
RANK_TILE = 512
ROW_TILE = 256
VMEM_LIMIT = 56 * 1024 * 1024
NEG_BIG = -1e30

F32 = jnp.float32
BF16 = jnp.bfloat16


def _dot(a, b):
    return jnp.dot(a, b, preferred_element_type=F32)


def _ln(x):
    mu = jnp.mean(x, axis=-1, keepdims=True)
    xc = x - mu
    var = jnp.mean(xc * xc, axis=-1, keepdims=True)
    return xc * lax.rsqrt(var + LN_EPS)


def _sigmoid(x):
    return 1.0 / (1.0 + jnp.exp(-x))


def _log_sigmoid(x):
    return jnp.minimum(x, 0.0) - jnp.log(1.0 + jnp.exp(-jnp.abs(x)))


def _split3(x):
    hi = x.astype(BF16)
    r1 = x - hi.astype(F32)
    mid = r1.astype(BF16)
    lo = (r1 - mid.astype(F32)).astype(BF16)
    return hi, mid, lo


def _const_spec(shape):
    nd = len(shape)
    return pl.BlockSpec(shape, lambda *_: (0,) * nd, pipeline_mode=pl.Buffered(1))


def _ada_kernel(c_ref, w_ref, b_ref, o_ref):
    c = c_ref[...]
    s = c * _sigmoid(c)
    o_ref[...] = _dot(s.astype(BF16), w_ref[...].astype(BF16)) + b_ref[...]


def _ada(c, w_ada, b_ada):
    nb, d = c.shape
    n_out = w_ada.shape[1]
    blk = D_MODEL
    return pl.pallas_call(
        _ada_kernel,
        grid=(n_out // blk,),
        in_specs=[
            pl.BlockSpec((nb, d), lambda i: (0, 0)),
            pl.BlockSpec((d, blk), lambda i: (0, i)),
            pl.BlockSpec((1, blk), lambda i: (0, i)),
        ],
        out_specs=pl.BlockSpec((nb, blk), lambda i: (0, i)),
        out_shape=jax.ShapeDtypeStruct((nb, n_out), F32),
        name="ada_mod",
    )(c, w_ada, b_ada.reshape(1, n_out))


def _mixer_kernel(x_ref, mod_ref, hist0_ref, c0_ref, n0_ref, m0_ref,
                  wu_ref, wq_ref, wk_ref, wv_ref, wo_ref, wga_ref, wgb_ref, wif_ref, bif_ref,
                  wgrp_ref, pscale_ref, mhg_ref, wpa_ref, wpb_ref, wout_ref, ln1g_ref, ln1b_ref,
                  wr_ref, br_ref,
                  x1_ref, h2_ref, eidx_ref, gate_ref, hist_ref, c_ref, n_ref, m_ref,
                  c_s, n_s, m_s, pbuf, *, T, pos0):
    j = pl.program_id(1)
    nj = pl.num_programs(1)

    @pl.when(j == 0)
    def _():
        c_s[...] = c0_ref[0]
        n_s[...] = n0_ref[0]
        m_s[...] = m0_ref[0]
        pbuf[0:1, :] = jnp.zeros((1, D_POOL), F32)
        pbuf[1:HIST_ROWS, :] = hist0_ref[0]

    x = x_ref[0]
    mod = mod_ref[0]
    sh1, sc1, g1, sh2, sc2, g2 = [mod[i:i + 1] for i in range(6)]
    h1b = (_ln(x) * (1.0 + sc1) + sh1).astype(BF16)

    u = _dot(h1b, wu_ref[...])
    pbuf[HIST_ROWS:HIST_ROWS + T, :] = u
    pos = pos0 + j * T + lax.broadcasted_iota(jnp.int32, (T, 1), 0)
    mixed = []
    for g, w in enumerate(POOL_WINDOWS):
        cols = slice(g * POOL_GROUP, (g + 1) * POOL_GROUP)
        win = pbuf[HIST_ROWS:HIST_ROWS + T, cols]
        for i in range(1, w):
            win = win + pbuf[HIST_ROWS - i:HIST_ROWS - i + T, cols]
        cnt = jnp.minimum(w, pos + 1).astype(F32)
        pooled = win / cnt - u[:, cols]
        mixed.append(_dot(pooled.astype(BF16), wgrp_ref[g]))
    a = jnp.concatenate(mixed, axis=1) * pscale_ref[...]
    pbuf[0:HIST_ROWS, :] = pbuf[T:T + HIST_ROWS, :]

    gif = _dot(h1b, wif_ref[...]) + bif_ref[...]
    gif_t = gif.T
    lf_c = _log_sigmoid(gif)
    lf_r = _log_sigmoid(gif_t)
    row = lax.broadcasted_iota(jnp.int32, (T, T), 0)
    col = lax.broadcasted_iota(jnp.int32, (T, T), 1)
    causal = col <= row
    tri = causal.astype(BF16)
    tri_t = (row <= col).astype(BF16)
    b_c = sum(_dot(tri, p) for p in _split3(lf_c))
    b_r = sum(_dot(p, tri_t) for p in _split3(lf_r))

    q = _dot(h1b, wq_ref[...])
    k = _dot(h1b, wk_ref[...]) * (HEAD_DK ** -0.5)
    v = _dot(h1b, wv_ref[...])
    m_all = m_s[...]
    lane = lax.broadcasted_iota(jnp.int32, (1, LANES), 1)
    m_next = m_all
    heads = []
    for h in range(N_HEADS):
        sl = slice(h * HEAD_DK, (h + 1) * HEAD_DK)
        qh, kh = q[:, sl], k[:, sl]
        qb, kb, vb = qh.astype(BF16), kh.astype(BF16), v[:, sl].astype(BF16)
        bc = b_c[:, N_HEADS + h:N_HEADS + h + 1]
        br = b_r[N_HEADS + h:N_HEADS + h + 1, :]
        ig_c = gif[:, h:h + 1]
        ig_r = gif_t[h:h + 1, :]
        m_prev = m_all[:, h:h + 1]
        log_d = jnp.where(causal, bc - br + ig_r, -jnp.inf)
        log_inter = bc + m_prev
        m_t = jnp.maximum(log_inter, jnp.max(log_d, axis=1, keepdims=True))
        d = jnp.exp(log_d - m_t)
        s = lax.dot_general(qb, kb, (((1,), (1,)), ((), ())), preferred_element_type=F32) * d
        inter = jnp.exp(log_inter - m_t)
        c_h = c_s[h]
        n_h = n_s[h:h + 1, :]
        num = _dot(s.astype(BF16), vb) + inter * _dot(qb, c_h.astype(BF16))
        den = jnp.sum(s, axis=1, keepdims=True) + inter * jnp.sum(qh * n_h, axis=1, keepdims=True)
        heads.append(_ln(num / jnp.maximum(jnp.abs(den), jnp.exp(-m_t))))
        b_last = bc[T - 1:T, :]
        lw_c = b_last - bc + ig_c
        lw_r = b_last - br + ig_r
        m_new = jnp.maximum(b_last + m_prev, jnp.max(lw_r, axis=1, keepdims=True))
        decay = jnp.exp(b_last + m_prev - m_new)
        wk = kh * jnp.exp(lw_c - m_new)
        c_s[h] = decay * c_h + _dot(wk.T.astype(BF16), vb)
        n_s[h:h + 1, :] = decay * n_h + jnp.sum(wk, axis=0, keepdims=True)
        m_next = jnp.where(lane == h, m_new, m_next)
    m_s[...] = m_next

    hm = jnp.concatenate(heads, axis=1) * mhg_ref[...]
    hm = hm * _sigmoid(_dot(h1b, wo_ref[...]))
    merged = (_sigmoid(_dot(h1b, wga_ref[...])) * _dot(a.astype(BF16), wpa_ref[...])
              + _sigmoid(_dot(h1b, wgb_ref[...])) * _dot(hm.astype(BF16), wpb_ref[...]))
    mix = _dot(merged.astype(BF16), wout_ref[...])
    x1 = _ln(DEEPNORM_ALPHA * x + g1 * mix) * ln1g_ref[...] + ln1b_ref[...]
    h2 = _ln(x1) * (1.0 + sc2) + sh2
    x1_ref[0] = x1
    h2_ref[0] = h2

    logits = _dot(h2.astype(BF16), wr_ref[...]) + br_ref[...]
    lt = logits.T[0:N_EXPERTS, :]
    eio = lax.broadcasted_iota(jnp.int32, (N_EXPERTS, T), 0)
    vals, idxs = [], []
    for _ in range(TOP_K):
        mx = jnp.max(lt, axis=0, keepdims=True)
        idx = jnp.min(jnp.where(lt == mx, eio, N_EXPERTS), axis=0, keepdims=True)
        vals.append(mx)
        idxs.append(idx)
        lt = jnp.where(eio == idx, -jnp.inf, lt)
    ex = [jnp.exp(vk - vals[0]) for vk in vals]
    tot = ex[0] + ex[1] + ex[2] + ex[3]
    eidx_ref[0] = jnp.concatenate(idxs, axis=0)
    gate_ref[0] = jnp.concatenate([e / tot for e in ex], axis=0)

    @pl.when(j == nj - 1)
    def _():
        hist_ref[0] = pbuf[1:HIST_ROWS, :]
        c_ref[0] = c_s[...]
        n_ref[0] = n_s[...]
        m_ref[0] = m_s[...]


def _mixer(x, mod, hist0, c0, n0, m0, wts, *, T, pos0):
    B, L, D = x.shape
    nj = L // T
    N = B * L
    kern = functools.partial(_mixer_kernel, T=T, pos0=pos0)
    seq_map = lambda b, j: (b, j, 0)
    st3 = lambda b, j: (b, 0, 0)
    st4 = lambda b, j: (b, 0, 0, 0)
    in_specs = [
        pl.BlockSpec((1, T, D), seq_map),
        pl.BlockSpec((1, 6, D), st3),
        pl.BlockSpec((1, POOL_HIST, D_POOL), st3),
        pl.BlockSpec((1, N_HEADS, HEAD_DK, HEAD_DV), st4),
        pl.BlockSpec((1, N_HEADS, HEAD_DK), st3),
        pl.BlockSpec((1, 1, LANES), st3),
    ] + [_const_spec(w.shape) for w in wts]
    out_specs = [
        pl.BlockSpec((1, T, D), seq_map),
        pl.BlockSpec((1, T, D), seq_map),
        pl.BlockSpec((1, TOP_K, T), lambda b, j: (b * nj + j, 0, 0)),
        pl.BlockSpec((1, TOP_K, T), lambda b, j: (b * nj + j, 0, 0)),
        pl.BlockSpec((1, POOL_HIST, D_POOL), st3),
        pl.BlockSpec((1, N_HEADS, HEAD_DK, HEAD_DV), st4),
        pl.BlockSpec((1, N_HEADS, HEAD_DK), st3),
        pl.BlockSpec((1, 1, LANES), st3),
    ]
    out_shape = [
        jax.ShapeDtypeStruct((B, L, D), F32),
        jax.ShapeDtypeStruct((B, L, D), F32),
        jax.ShapeDtypeStruct((N // T, TOP_K, T), jnp.int32),
        jax.ShapeDtypeStruct((N // T, TOP_K, T), F32),
        jax.ShapeDtypeStruct((B, POOL_HIST, D_POOL), F32),
        jax.ShapeDtypeStruct((B, N_HEADS, HEAD_DK, HEAD_DV), F32),
        jax.ShapeDtypeStruct((B, N_HEADS, HEAD_DK), F32),
        jax.ShapeDtypeStruct((B, 1, LANES), F32),
    ]
    scratch = [
        pltpu.VMEM((N_HEADS, HEAD_DK, HEAD_DV), F32),
        pltpu.VMEM((N_HEADS, HEAD_DK), F32),
        pltpu.VMEM((1, LANES), F32),
        pltpu.VMEM((T + HIST_ROWS, D_POOL), F32),
    ]
    return pl.pallas_call(
        kern,
        grid=(B, nj),
        in_specs=in_specs,
        out_specs=out_specs,
        out_shape=out_shape,
        scratch_shapes=scratch,
        compiler_params=pltpu.CompilerParams(
            dimension_semantics=("arbitrary", "arbitrary"), vmem_limit_bytes=VMEM_LIMIT),
        name="mixer",
    )(x, mod, hist0, c0, n0, m0, *wts)


def _rank_kernel(e_ref, rank_ref, cnt_ref, base_s, *, T):
    @pl.when(pl.program_id(0) == 0)
    def _():
        base_s[...] = jnp.zeros_like(base_s)

    e = e_ref[...]
    eio = lax.broadcasted_iota(jnp.int32, (N_EXPERTS, T), 0)
    hits = [eio == e[k:k + 1, :] for k in range(TOP_K)]
    oh = sum(h.astype(F32) for h in hits)
    row = lax.broadcasted_iota(jnp.int32, (T, T), 0)
    col = lax.broadcasted_iota(jnp.int32, (T, T), 1)
    before = _dot(oh.astype(BF16), (row < col).astype(BF16))
    tot = before + base_s[:, 0:1]
    ranks = [jnp.sum(jnp.where(h, tot, 0.0), axis=0, keepdims=True) for h in hits]
    rank_ref[...] = jnp.concatenate(ranks, axis=0).astype(jnp.int32)
    base_s[...] = base_s[...] + jnp.sum(oh, axis=1, keepdims=True)
    cnt_ref[...] = base_s[...].astype(jnp.int32)


def _ranks(eidx):
    K, N = eidx.shape
    T = RANK_TILE
    return pl.pallas_call(
        functools.partial(_rank_kernel, T=T),
        grid=(N // T,),
        in_specs=[pl.BlockSpec((K, T), lambda i: (0, i))],
        out_specs=[pl.BlockSpec((K, T), lambda i: (0, i)),
                   pl.BlockSpec((N_EXPERTS, LANES), lambda i: (0, 0))],
        out_shape=[jax.ShapeDtypeStruct((K, N), jnp.int32),
                   jax.ShapeDtypeStruct((N_EXPERTS, LANES), jnp.int32)],
        scratch_shapes=[pltpu.VMEM((N_EXPERTS, LANES), F32)],
        compiler_params=pltpu.CompilerParams(dimension_semantics=("arbitrary",)),
        name="expert_rank",
    )(eidx)


def _dispatch_kernel(fill_ref, dest_ref, ha_ref, hb_ref, xs_ref, zbuf, sem, *, T, tiles_a):

    def zero_fill(finish):
        def go(off, size):
            cp = pltpu.make_async_copy(zbuf.at[pl.ds(0, size)], xs_ref.at[pl.ds(off, size)], sem)
            cp.wait() if finish else cp.start()

        for e in range(N_EXPERTS):
            off, pad = fill_ref[e], fill_ref[N_EXPERTS + e]
            head = pad & (SUBLANES - 1)
            for r in range(SUBLANES - 1):
                pl.when(r < head)(functools.partial(go, off + r, 1))
            off = pl.multiple_of(off + head, SUBLANES)
            for bit in range(SUBLANES.bit_length() - 1, MOE_ROWS.bit_length() - 1):
                size = 1 << bit
                pl.when((pad & size) != 0)(functools.partial(go, off, size))
                off = pl.multiple_of(off + (pad & size), SUBLANES)

        def tail(blk, c):
            go(pl.multiple_of(blk * MOE_ROWS, MOE_ROWS), MOE_ROWS)
            return c
        lax.fori_loop(fill_ref[2 * N_EXPERTS], xs_ref.shape[0] // MOE_ROWS, tail, 0)

    i = pl.program_id(0)

    @pl.when(i == 0)
    def _():
        zbuf[...] = jnp.zeros_like(zbuf)
        zero_fill(False)
        zero_fill(True)

    def scatter_rows(h_ref):
        def body(t, c):
            for k in range(TOP_K):
                pltpu.make_async_copy(h_ref.at[pl.ds(t, 1)], xs_ref.at[pl.ds(dest_ref[k, t], 1)], sem).start()
            return c

        lax.fori_loop(0, T, body, 0, unroll=8)
        for _ in range(TOP_K):
            pltpu.make_async_copy(h_ref, xs_ref.at[pl.ds(0, T)], sem).wait()

    pl.when(i < tiles_a)(functools.partial(scatter_rows, ha_ref))
    pl.when(i >= tiles_a)(functools.partial(scatter_rows, hb_ref))


def _dispatch(fill, dest, h2_a, h2_b, n_rows, *, T):
    D = h2_a.shape[1]
    tiles_a, tiles_b = h2_a.shape[0] // T, h2_b.shape[0] // T
    grid_spec = pltpu.PrefetchScalarGridSpec(
        num_scalar_prefetch=1,
        grid=(tiles_a + tiles_b,),
        in_specs=[pl.BlockSpec((TOP_K, T), lambda i, fill: (0, i), memory_space=pltpu.SMEM),
                  pl.BlockSpec((T, D), lambda i, fill: (jnp.minimum(i, tiles_a - 1), 0)),
                  pl.BlockSpec((T, D), lambda i, fill: (jnp.maximum(i - tiles_a, 0), 0))],
        out_specs=pl.BlockSpec(memory_space=pl.ANY),
        scratch_shapes=[pltpu.VMEM((MOE_ROWS, D), h2_a.dtype), pltpu.SemaphoreType.DMA(())],
    )
    return pl.pallas_call(
        functools.partial(_dispatch_kernel, T=T, tiles_a=tiles_a),
        grid_spec=grid_spec,
        out_shape=jax.ShapeDtypeStruct((n_rows, D), h2_a.dtype),
        compiler_params=pltpu.CompilerParams(dimension_semantics=("arbitrary",)),
        name="dispatch",
    )(fill, dest, h2_a, h2_b)


def _moe_kernel(be_ref, nu_ref, xs_ref, wgu_ref, bgu_ref, wd_ref, bd_ref, y_ref, wgu_s, wd_s):
    i = pl.program_id(0)
    used = i < nu_ref[0]
    new_expert = jnp.logical_or(i == 0, be_ref[i] != be_ref[jnp.maximum(i - 1, 0)])

    @pl.when(jnp.logical_and(used, new_expert))
    def _():
        def cast_rows(r, c):
            rows = pl.ds(pl.multiple_of(r * LANES, LANES), LANES)
            wgu_s[rows, :] = wgu_ref[0, rows, :].astype(BF16)
            wd_s[rows, :] = wd_ref[0, rows, :].astype(BF16)
            return c
        lax.fori_loop(0, D_FF // LANES, cast_rows, 0)

    @pl.when(used)
    def _():
        x = xs_ref[...].astype(BF16)
        bgu = bgu_ref[0]
        acc = None
        for c in range(D_FF // FF_CHUNK):
            cs = slice(c * FF_CHUNK, (c + 1) * FF_CHUNK)
            ls = slice(D_FF + c * FF_CHUNK, D_FF + (c + 1) * FF_CHUNK)
            glu = jnp.minimum(_dot(x, wgu_s[:, cs]) + bgu[:, cs], SWIGLU_LIMIT)
            lin = jnp.clip(_dot(x, wgu_s[:, ls]) + bgu[:, ls], -SWIGLU_LIMIT, SWIGLU_LIMIT)
            act = glu * _sigmoid(SWIGLU_ALPHA * glu) * (lin + 1.0)
            part = _dot(act.astype(BF16), wd_s[cs, :])
            acc = part if acc is None else acc + part
        y_ref[...] = acc + bd_ref[0]

    @pl.when(jnp.logical_not(used))
    def _():
        def zero_rows(r, c):
            rows = pl.ds(pl.multiple_of(r * LANES, LANES), LANES)
            y_ref[rows, :] = jnp.zeros((LANES, y_ref.shape[1]), F32)
            return c
        lax.fori_loop(0, y_ref.shape[0] // LANES, zero_rows, 0)


def _moe(block_e, n_used, xs, w_gu, b_gu, w_down, b_down):
    P, D = xs.shape
    R = MOE_ROWS
    nb = P // R
    xs_map = lambda i, be, nu: (jnp.minimum(i, nu[0] - 1), 0)
    w_map = lambda i, be, nu: (be[i], 0, 0)
    grid_spec = pltpu.PrefetchScalarGridSpec(
        num_scalar_prefetch=2,
        grid=(nb,),
        in_specs=[
            pl.BlockSpec((R, D), xs_map),
            pl.BlockSpec((1, D, 2 * D_FF), w_map),
            pl.BlockSpec((1, 1, 2 * D_FF), w_map),
            pl.BlockSpec((1, D_FF, D), w_map),
            pl.BlockSpec((1, 1, D), w_map),
        ],
        out_specs=pl.BlockSpec((R, D), lambda i, be, nu: (i, 0)),
        scratch_shapes=[pltpu.VMEM((D, 2 * D_FF), BF16), pltpu.VMEM((D_FF, D), BF16)],
    )
    return pl.pallas_call(
        _moe_kernel,
        grid_spec=grid_spec,
        out_shape=jax.ShapeDtypeStruct((P, D), F32),
        compiler_params=pltpu.CompilerParams(
            dimension_semantics=("arbitrary",), vmem_limit_bytes=VMEM_LIMIT),
        name="moe_experts",
    )(block_e, n_used, xs, w_gu, b_gu, w_down, b_down)


def _combine_kernel(dcur_ref, dnxt_ref, gate_ref, x1_ref, mod_ref, g_ref, b_ref, y_hbm,
                    x2_ref, ybuf, sem, *, T):
    nj = pl.num_programs(1)
    step = pl.program_id(0) * nj + pl.program_id(1)
    n_steps = pl.num_programs(0) * nj
    slot = step % 2

    def issue(dref, s):
        def body(t, c):
            for k in range(TOP_K):
                pltpu.make_async_copy(y_hbm.at[pl.ds(dref[k, t], 1)], ybuf.at[s, k, pl.ds(t, 1)],
                                      sem.at[s]).start()
            return c
        lax.fori_loop(0, T, body, 0, unroll=8)

    @pl.when(step == 0)
    def _():
        issue(dcur_ref, 0)

    @pl.when(step + 1 < n_steps)
    def _():
        issue(dnxt_ref, 1 - slot)

    for k in range(TOP_K):
        pltpu.make_async_copy(y_hbm.at[pl.ds(0, T)], ybuf.at[slot, k], sem.at[slot]).wait()

    gates = jnp.concatenate([gate_ref[...], jnp.zeros((LANES - TOP_K, T), F32)], axis=0).T
    ffn = gates[:, 0:1] * ybuf[slot, 0]
    for k in range(1, TOP_K):
        ffn = ffn + gates[:, k:k + 1] * ybuf[slot, k]
    bt, lt, d = x1_ref.shape
    g2 = mod_ref[...][:, 5:6, :]
    z = DEEPNORM_ALPHA * x1_ref[...] + g2 * ffn.reshape(bt, lt, d)
    x2_ref[...] = _ln(z) * g_ref[...] + b_ref[...]


def _combine(dest, gates, x1, mod, ln_g, ln_b, y, *, T):
    B, L, D = x1.shape
    bt, lt = (1, T) if L >= T else (T // L, L)
    nj = L // lt
    n_tiles = (B // bt) * nj
    cur = lambda b, j: (0, b * nj + j)
    nxt = lambda b, j: (0, jnp.minimum(b * nj + j + 1, n_tiles - 1))
    return pl.pallas_call(
        functools.partial(_combine_kernel, T=T),
        grid=(B // bt, nj),
        in_specs=[
            pl.BlockSpec((TOP_K, T), cur, memory_space=pltpu.SMEM),
            pl.BlockSpec((TOP_K, T), nxt, memory_space=pltpu.SMEM),
            pl.BlockSpec((TOP_K, T), cur),
            pl.BlockSpec((bt, lt, D), lambda b, j: (b, j, 0)),
            pl.BlockSpec((bt, 6, D), lambda b, j: (b, 0, 0)),
            pl.BlockSpec((1, D), lambda b, j: (0, 0)),
            pl.BlockSpec((1, D), lambda b, j: (0, 0)),
            pl.BlockSpec(memory_space=pl.ANY),
        ],
        out_specs=pl.BlockSpec((bt, lt, D), lambda b, j: (b, j, 0)),
        out_shape=jax.ShapeDtypeStruct((B, L, D), F32),
        scratch_shapes=[pltpu.VMEM((2, TOP_K, T, D), F32), pltpu.SemaphoreType.DMA((2,))],
        compiler_params=pltpu.CompilerParams(
            dimension_semantics=("arbitrary", "arbitrary"), vmem_limit_bytes=VMEM_LIMIT),
        name="combine",
    )(dest, dest, gates, x1, mod, ln_g, ln_b, y)


def _routing_tables(eidx, rank, counts, n_blocks):
    R = MOE_ROWS
    padded = (counts + R - 1) // R * R
    pad_end = jnp.cumsum(padded)
    pad_start = pad_end - padded
    experts = jnp.arange(N_EXPERTS, dtype=jnp.int32)
    start_of = jnp.sum(jnp.where(eidx[..., None] == experts, pad_start, 0), axis=-1)
    dest = (rank + start_of).astype(jnp.int32)
    blk0 = jnp.arange(n_blocks, dtype=jnp.int32) * R
    block_e = jnp.sum((blk0[:, None] >= pad_end[None, :]).astype(jnp.int32), axis=1)
    block_e = jnp.minimum(block_e, N_EXPERTS - 1)
    n_used = (pad_end[-1:] // R).astype(jnp.int32)
    fill = jnp.concatenate([pad_start + counts, padded - counts, n_used]).astype(jnp.int32)
    return dest, block_e, n_used, fill


def _mix_group(x, mod, hist0, c0, n0, m0, mixer_w, *, T, pos0):
    B, L, _ = x.shape
    N = B * L
    m0p = jnp.pad(m0, ((0, 0), (0, LANES - N_HEADS))).reshape(B, 1, LANES)
    x1, h2, eidx, gates, hist, c_new, n_new, m_new = _mixer(
        x, mod, hist0, c0, n0, m0p, mixer_w, T=T, pos0=pos0)
    eidx = eidx.transpose(1, 0, 2).reshape(TOP_K, N)
    gates = gates.transpose(1, 0, 2).reshape(TOP_K, N)
    states = (hist, c_new, n_new, m_new.reshape(B, LANES)[:, :N_HEADS])
    return x1, h2.reshape(N, -1), eidx, gates, states


def kernel(x_prompt, x_sample, state_pool, state_mlstm_C, state_mlstm_n, state_mlstm_m, c_prompt, c_sample,
           w_ada, b_ada, w_in, b_if, w_group, pool_scale, mh_norm_g, w_proj_a, w_proj_b, w_out,
           ln1_g, ln1_b, w_router, b_router, w_gu, b_gu, w_down, b_down, ln2_g, ln2_b):
    assert w_ada.shape[0] == DEPTH == 1
    l = 0
    bp, bs = x_prompt.shape[0], x_sample.shape[0]
    dt = x_prompt.dtype
    row = lambda a: a.reshape(1, -1)

    wi = w_in[l]
    o0 = 0
    parts = []
    for width in (D_POOL, N_HEADS * HEAD_DK, N_HEADS * HEAD_DK, N_HEADS * HEAD_DV, N_HEADS * HEAD_DV):
        parts.append(wi[:, o0:o0 + width].astype(BF16))
        o0 += width
    w_if = jnp.pad(wi[:, o0:o0 + 2 * N_HEADS], ((0, 0), (0, LANES - 2 * N_HEADS))).astype(BF16)
    o0 += 2 * N_HEADS
    w_ga = wi[:, o0:o0 + D_MODEL].astype(BF16)
    w_gb = wi[:, o0 + D_MODEL:o0 + 2 * D_MODEL].astype(BF16)
    w_u, w_q, w_k, w_v, w_o = parts
    b_if_row = jnp.pad(b_if[l], (0, LANES - 2 * N_HEADS)).reshape(1, LANES)
    w_r = jnp.pad(w_router[l], ((0, 0), (0, LANES - N_EXPERTS))).astype(BF16)
    b_r = jnp.pad(b_router[l], (0, LANES - N_EXPERTS), constant_values=NEG_BIG).reshape(1, LANES)
    mixer_w = (w_u, w_q, w_k, w_v, w_o, w_ga, w_gb, w_if, b_if_row,
               w_group[l].astype(BF16), row(pool_scale[l]), row(mh_norm_g[l]),
               w_proj_a[l].astype(BF16), w_proj_b[l].astype(BF16), w_out[l].astype(BF16),
               row(ln1_g[l]), row(ln1_b[l]), w_r, b_r)
    moe_w = (w_gu[l], b_gu[l].reshape(N_EXPERTS, 1, 2 * D_FF),
             w_down[l], b_down[l].reshape(N_EXPERTS, 1, D_MODEL))

    mod = _ada(jnp.concatenate([c_prompt, c_sample], axis=0), w_ada[l], b_ada[l])
    mod = mod.reshape(bp + bs, 6, D_MODEL)

    zeros = lambda *s: jnp.zeros(s, dt)
    x1_p, h2_p, eidx_p, gates_p, st_p = _mix_group(
        x_prompt, mod[:bp], zeros(bp, POOL_HIST, D_POOL), zeros(bp, N_HEADS, HEAD_DK, HEAD_DV),
        zeros(bp, N_HEADS, HEAD_DK), zeros(bp, N_HEADS), mixer_w, T=min(256, x_prompt.shape[1]), pos0=0)
    x1_s, h2_s, eidx_s, gates_s, st_s = _mix_group(
        x_sample, mod[bp:], state_pool[l], state_mlstm_C[l], state_mlstm_n[l], state_mlstm_m[l],
        mixer_w, T=min(256, x_sample.shape[1]), pos0=PAST_LEN)

    n_p, n_s = h2_p.shape[0], h2_s.shape[0]
    eidx = jnp.concatenate([eidx_p, eidx_s], axis=1)
    rank, cnt = _ranks(eidx)
    n_blocks = -(-((n_p + n_s) * TOP_K) // MOE_ROWS) + N_EXPERTS
    dest, block_e, n_used, fill = _routing_tables(eidx, rank, cnt[:, 0], n_blocks)
    dest_p, dest_s = dest[:, :n_p], dest[:, n_p:]
    xs = _dispatch(fill, dest, h2_p, h2_s, n_blocks * MOE_ROWS, T=ROW_TILE)
    y = _moe(block_e, n_used, xs, *moe_w)
    yp = _combine(dest_p, gates_p, x1_p, mod[:bp], row(ln2_g[l]), row(ln2_b[l]), y, T=ROW_TILE)
    ys = _combine(dest_s, gates_s, x1_s, mod[bp:], row(ln2_g[l]), row(ln2_b[l]), y, T=ROW_TILE)
    st = lambda a: a[None]
    return (yp, ys) + tuple(st(a) for a in st_p) + tuple(st(a) for a in st_s)
```

```python
import functools

import jax
import jax.numpy as jnp
from jax import lax
from jax.experimental import pallas as pl
from jax.experimental.pallas import tpu as pltpu

D_MODEL = 1024
DEPTH = 1
D_POOL = 512
POOL_WINDOWS = (2, 4, 8, 16)
POOL_GROUP = 128
POOL_HIST = 15
N_HEADS = 4
HEAD_DK = 256
HEAD_DV = 256
N_EXPERTS = 32
TOP_K = 4
D_FF = 1024
SWIGLU_ALPHA = 1.702
SWIGLU_LIMIT = 7.0
LN_EPS = 1e-5
DEEPNORM_ALPHA = (2.0 * DEPTH) ** 0.25
PAST_LEN = 4096

LANES = 128
SUBLANES = 8
HIST_ROWS = POOL_HIST + 1
MOE_ROWS = 512
FF_CHUNK = 256
ROW_TILE = 256
SORT_ROWS = TOP_K * ROW_TILE + N_EXPERTS * SUBLANES
PIECE_LANES = -(-(SORT_ROWS // SUBLANES) // LANES) * LANES
VMEM_LIMIT = 56 * 1024 * 1024
NEG_BIG = -1e30

F32 = jnp.float32
BF16 = jnp.bfloat16


def _dot(a, b):
    return jnp.dot(a, b, preferred_element_type=F32)


def _ln(x):
    mu = jnp.mean(x, axis=-1, keepdims=True)
    xc = x - mu
    var = jnp.mean(xc * xc, axis=-1, keepdims=True)
    return xc * lax.rsqrt(var + LN_EPS)


def _sigmoid(x):
    return 1.0 / (1.0 + jnp.exp(-x))


def _log_sigmoid(x):
    return jnp.minimum(x, 0.0) - jnp.log(1.0 + jnp.exp(-jnp.abs(x)))


def _split3(x):
    hi = x.astype(BF16)
    r1 = x - hi.astype(F32)
    mid = r1.astype(BF16)
    lo = (r1 - mid.astype(F32)).astype(BF16)
    return hi, mid, lo


def _const_spec(shape):
    nd = len(shape)
    return pl.BlockSpec(shape, lambda *_: (0,) * nd, pipeline_mode=pl.Buffered(1))


def _ada_kernel(c_ref, w_ref, b_ref, o_ref):
    c = c_ref[...]
    s = c * _sigmoid(c)
    o_ref[...] = _dot(s.astype(BF16), w_ref[...].astype(BF16)) + b_ref[...]


def _ada(c, w_ada, b_ada):
    nb, d = c.shape
    n_out = w_ada.shape[1]
    blk = D_MODEL
    return pl.pallas_call(
        _ada_kernel,
        grid=(n_out // blk,),
        in_specs=[
            pl.BlockSpec((nb, d), lambda i: (0, 0)),
            pl.BlockSpec((d, blk), lambda i: (0, i)),
            pl.BlockSpec((1, blk), lambda i: (0, i)),
        ],
        out_specs=pl.BlockSpec((nb, blk), lambda i: (0, i)),
        out_shape=jax.ShapeDtypeStruct((nb, n_out), F32),
        name="ada_mod",
    )(c, w_ada, b_ada.reshape(1, n_out))


def _mixer_kernel(x_ref, mod_ref, hist0_ref, c0_ref, n0_ref, m0_ref,
                  wu_ref, wq_ref, wk_ref, wv_ref, wo_ref, wga_ref, wgb_ref, wif_ref, bif_ref,
                  wgrp_ref, pscale_ref, mhg_ref, wpa_ref, wpb_ref, wout_ref, ln1g_ref, ln1b_ref,
                  wr_ref, br_ref,
                  x1_ref, h2_ref, eidx_ref, gate_ref, hist_ref, c_ref, n_ref, m_ref,
                  c_s, n_s, m_s, pbuf, *, T, pos0):
    j = pl.program_id(1)
    nj = pl.num_programs(1)

    @pl.when(j == 0)
    def _():
        c_s[...] = c0_ref[0]
        n_s[...] = n0_ref[0]
        m_s[...] = m0_ref[0]
        pbuf[0:1, :] = jnp.zeros((1, D_POOL), F32)
        pbuf[1:HIST_ROWS, :] = hist0_ref[0]

    x = x_ref[0]
    mod = mod_ref[0]
    sh1, sc1, g1, sh2, sc2, g2 = [mod[i:i + 1] for i in range(6)]
    h1b = (_ln(x) * (1.0 + sc1) + sh1).astype(BF16)

    u = _dot(h1b, wu_ref[...])
    pbuf[HIST_ROWS:HIST_ROWS + T, :] = u
    pos = pos0 + j * T + lax.broadcasted_iota(jnp.int32, (T, 1), 0)
    mixed = []
    for g, w in enumerate(POOL_WINDOWS):
        cols = slice(g * POOL_GROUP, (g + 1) * POOL_GROUP)
        win = pbuf[HIST_ROWS:HIST_ROWS + T, cols]
        for i in range(1, w):
            win = win + pbuf[HIST_ROWS - i:HIST_ROWS - i + T, cols]
        cnt = jnp.minimum(w, pos + 1).astype(F32)
        pooled = win / cnt - u[:, cols]
        mixed.append(_dot(pooled.astype(BF16), wgrp_ref[g]))
    a = jnp.concatenate(mixed, axis=1) * pscale_ref[...]
    pbuf[0:HIST_ROWS, :] = pbuf[T:T + HIST_ROWS, :]

    gif = _dot(h1b, wif_ref[...]) + bif_ref[...]
    gif_t = gif.T
    lf_c = _log_sigmoid(gif)
    lf_r = _log_sigmoid(gif_t)
    row = lax.broadcasted_iota(jnp.int32, (T, T), 0)
    col = lax.broadcasted_iota(jnp.int32, (T, T), 1)
    causal = col <= row
    tri = causal.astype(BF16)
    tri_t = (row <= col).astype(BF16)
    b_c = sum(_dot(tri, p) for p in _split3(lf_c))
    b_r = sum(_dot(p, tri_t) for p in _split3(lf_r))

    q = _dot(h1b, wq_ref[...])
    k = _dot(h1b, wk_ref[...]) * (HEAD_DK ** -0.5)
    v = _dot(h1b, wv_ref[...])
    m_all = m_s[...]
    lane = lax.broadcasted_iota(jnp.int32, (1, LANES), 1)
    m_next = m_all
    heads = []
    for h in range(N_HEADS):
        sl = slice(h * HEAD_DK, (h + 1) * HEAD_DK)
        qh, kh = q[:, sl], k[:, sl]
        qb, kb, vb = qh.astype(BF16), kh.astype(BF16), v[:, sl].astype(BF16)
        bc = b_c[:, N_HEADS + h:N_HEADS + h + 1]
        br = b_r[N_HEADS + h:N_HEADS + h + 1, :]
        ig_c = gif[:, h:h + 1]
        ig_r = gif_t[h:h + 1, :]
        m_prev = m_all[:, h:h + 1]
        log_d = jnp.where(causal, bc - br + ig_r, -jnp.inf)
        log_inter = bc + m_prev
        m_t = jnp.maximum(log_inter, jnp.max(log_d, axis=1, keepdims=True))
        d = jnp.exp(log_d - m_t)
        s = lax.dot_general(qb, kb, (((1,), (1,)), ((), ())), preferred_element_type=F32) * d
        inter = jnp.exp(log_inter - m_t)
        c_h = c_s[h]
        n_h = n_s[h:h + 1, :]
        num = _dot(s.astype(BF16), vb) + inter * _dot(qb, c_h.astype(BF16))
        den = jnp.sum(s, axis=1, keepdims=True) + inter * jnp.sum(qh * n_h, axis=1, keepdims=True)
        heads.append(_ln(num / jnp.maximum(jnp.abs(den), jnp.exp(-m_t))))
        b_last = bc[T - 1:T, :]
        lw_c = b_last - bc + ig_c
        lw_r = b_last - br + ig_r
        m_new = jnp.maximum(b_last + m_prev, jnp.max(lw_r, axis=1, keepdims=True))
        decay = jnp.exp(b_last + m_prev - m_new)
        wk = kh * jnp.exp(lw_c - m_new)
        c_s[h] = decay * c_h + _dot(wk.T.astype(BF16), vb)
        n_s[h:h + 1, :] = decay * n_h + jnp.sum(wk, axis=0, keepdims=True)
        m_next = jnp.where(lane == h, m_new, m_next)
    m_s[...] = m_next

    hm = jnp.concatenate(heads, axis=1) * mhg_ref[...]
    hm = hm * _sigmoid(_dot(h1b, wo_ref[...]))
    merged = (_sigmoid(_dot(h1b, wga_ref[...])) * _dot(a.astype(BF16), wpa_ref[...])
              + _sigmoid(_dot(h1b, wgb_ref[...])) * _dot(hm.astype(BF16), wpb_ref[...]))
    mix = _dot(merged.astype(BF16), wout_ref[...])
    x1 = _ln(DEEPNORM_ALPHA * x + g1 * mix) * ln1g_ref[...] + ln1b_ref[...]
    h2 = _ln(x1) * (1.0 + sc2) + sh2
    x1_ref[0] = x1
    h2b = h2.astype(BF16)
    h2_ref[0] = h2b

    logits = _dot(h2b, wr_ref[...]) + br_ref[...]
    lt = logits.T[0:N_EXPERTS, :]
    eio = lax.broadcasted_iota(jnp.int32, (N_EXPERTS, T), 0)
    vals, idxs = [], []
    for _ in range(TOP_K):
        mx = jnp.max(lt, axis=0, keepdims=True)
        idx = jnp.min(jnp.where(lt == mx, eio, N_EXPERTS), axis=0, keepdims=True)
        vals.append(mx)
        idxs.append(idx)
        lt = jnp.where(eio == idx, -jnp.inf, lt)
    ex = [jnp.exp(vk - vals[0]) for vk in vals]
    tot = ex[0] + ex[1] + ex[2] + ex[3]
    eidx_ref[0] = jnp.concatenate(idxs, axis=0)
    gate_ref[0] = jnp.concatenate([e / tot for e in ex], axis=0)

    @pl.when(j == nj - 1)
    def _():
        hist_ref[0] = pbuf[1:HIST_ROWS, :]
        c_ref[0] = c_s[...]
        n_ref[0] = n_s[...]
        m_ref[0] = m_s[...]


def _mixer(x, mod, hist0, c0, n0, m0, wts, *, T, pos0):
    B, L, D = x.shape
    nj = L // T
    N = B * L
    kern = functools.partial(_mixer_kernel, T=T, pos0=pos0)
    seq_map = lambda b, j: (b, j, 0)
    st3 = lambda b, j: (b, 0, 0)
    st4 = lambda b, j: (b, 0, 0, 0)
    in_specs = [
        pl.BlockSpec((1, T, D), seq_map),
        pl.BlockSpec((1, 6, D), st3),
        pl.BlockSpec((1, POOL_HIST, D_POOL), st3),
        pl.BlockSpec((1, N_HEADS, HEAD_DK, HEAD_DV), st4),
        pl.BlockSpec((1, N_HEADS, HEAD_DK), st3),
        pl.BlockSpec((1, 1, LANES), st3),
    ] + [_const_spec(w.shape) for w in wts]
    out_specs = [
        pl.BlockSpec((1, T, D), seq_map),
        pl.BlockSpec((1, T, D), seq_map),
        pl.BlockSpec((1, TOP_K, T), lambda b, j: (b * nj + j, 0, 0)),
        pl.BlockSpec((1, TOP_K, T), lambda b, j: (b * nj + j, 0, 0)),
        pl.BlockSpec((1, POOL_HIST, D_POOL), st3),
        pl.BlockSpec((1, N_HEADS, HEAD_DK, HEAD_DV), st4),
        pl.BlockSpec((1, N_HEADS, HEAD_DK), st3),
        pl.BlockSpec((1, 1, LANES), st3),
    ]
    out_shape = [
        jax.ShapeDtypeStruct((B, L, D), F32),
        jax.ShapeDtypeStruct((B, L, D), BF16),
        jax.ShapeDtypeStruct((N // T, TOP_K, T), jnp.int32),
        jax.ShapeDtypeStruct((N // T, TOP_K, T), F32),
        jax.ShapeDtypeStruct((B, POOL_HIST, D_POOL), F32),
        jax.ShapeDtypeStruct((B, N_HEADS, HEAD_DK, HEAD_DV), F32),
        jax.ShapeDtypeStruct((B, N_HEADS, HEAD_DK), F32),
        jax.ShapeDtypeStruct((B, 1, LANES), F32),
    ]
    scratch = [
        pltpu.VMEM((N_HEADS, HEAD_DK, HEAD_DV), F32),
        pltpu.VMEM((N_HEADS, HEAD_DK), F32),
        pltpu.VMEM((1, LANES), F32),
        pltpu.VMEM((T + HIST_ROWS, D_POOL), F32),
    ]
    return pl.pallas_call(
        kern,
        grid=(B, nj),
        in_specs=in_specs,
        out_specs=out_specs,
        out_shape=out_shape,
        scratch_shapes=scratch,
        compiler_params=pltpu.CompilerParams(
            dimension_semantics=("arbitrary", "arbitrary"), vmem_limit_bytes=VMEM_LIMIT),
        name="mixer",
    )(x, mod, hist0, c0, n0, m0, *wts)


def _pad8(x):
    return jnp.floor((x + (SUBLANES - 1)) * (1.0 / SUBLANES)) * SUBLANES


def _expert_counts_row(oh, T):
    ones = jnp.ones((SUBLANES, T), BF16)
    return lax.dot_general(ones, oh.astype(BF16), (((1,), (1,)), ((), ())), preferred_element_type=F32)[0:1]


def _local_positions(e, T):
    eio = lax.broadcasted_iota(jnp.int32, (N_EXPERTS, T), 0)
    hits = [eio == e[k:k + 1, :] for k in range(TOP_K)]
    oh = sum(h.astype(F32) for h in hits)
    row = lax.broadcasted_iota(jnp.int32, (T, T), 0)
    col = lax.broadcasted_iota(jnp.int32, (T, T), 1)
    before = _dot(oh.astype(BF16), (row < col).astype(BF16))
    len8 = _pad8(_expert_counts_row(oh, T))
    er = lax.broadcasted_iota(jnp.int32, (N_EXPERTS, N_EXPERTS), 0)
    ec = lax.broadcasted_iota(jnp.int32, (N_EXPERTS, N_EXPERTS), 1)
    start = jnp.sum(jnp.where(ec < er, len8, 0.0), axis=1, keepdims=True)
    base = before + start
    return [jnp.sum(jnp.where(h, base, 0.0), axis=0, keepdims=True) for h in hits]


def _count_kernel(e_ref, cnt_ref, *, T, tiles):
    eio = lax.broadcasted_iota(jnp.int32, (LANES, T), 0)
    for b in range(tiles):
        e = e_ref[:, b * T:(b + 1) * T]
        oh = sum((eio == e[k:k + 1, :]).astype(F32) for k in range(TOP_K))
        cnt_ref[b] = _expert_counts_row(oh, T).astype(jnp.int32)


def _tile_counts(eidx, *, T):
    K, N = eidx.shape
    n_tiles = N // T
    tiles = next(t for t in (8, 4, 2, 1) if n_tiles % t == 0)
    return pl.pallas_call(
        functools.partial(_count_kernel, T=T, tiles=tiles),
        grid=(n_tiles // tiles,),
        in_specs=[pl.BlockSpec((K, tiles * T), lambda i: (0, i))],
        out_specs=pl.BlockSpec((tiles, 1, LANES), lambda i: (i, 0, 0)),
        out_shape=jax.ShapeDtypeStruct((n_tiles, 1, LANES), jnp.int32),
        name="tile_counts",
    )(eidx)


def _start_piece_copies(rows_ref, make_copy):
    for p in range(SORT_ROWS // SUBLANES):
        make_copy(p * SUBLANES, pl.multiple_of(rows_ref[0, 0, p], SUBLANES)).start()


def _dispatch_kernel(fill_ref, rows_ref, e_ref, ha_ref, hb_ref, xs_ref, sbuf, zbuf, sem, zsem, *, T, tiles_a):
    i = pl.program_id(0)
    n_tiles = pl.num_programs(0) - 1
    slot = i % 2

    def zero_fill(finish):
        def go(off, size):
            cp = pltpu.make_async_copy(zbuf.at[pl.ds(0, size)], xs_ref.at[pl.ds(off, size)], zsem)
            cp.wait() if finish else cp.start()

        for e in range(N_EXPERTS):
            off, pad = pl.multiple_of(fill_ref[e], SUBLANES), fill_ref[N_EXPERTS + e]
            for bit in range(SUBLANES.bit_length() - 1, MOE_ROWS.bit_length() - 1):
                size = 1 << bit
                pl.when((pad & size) != 0)(functools.partial(go, off, size))
                off = pl.multiple_of(off + (pad & size), SUBLANES)

        def tail(blk, c):
            go(pl.multiple_of(blk * MOE_ROWS, MOE_ROWS), MOE_ROWS)
            return c
        lax.fori_loop(fill_ref[2 * N_EXPERTS], xs_ref.shape[0] // MOE_ROWS, tail, 0)

    @pl.when(i == 0)
    def _():
        zbuf[...] = jnp.zeros_like(zbuf)
        zero_fill(False)
        zero_fill(True)
        sbuf[1] = jnp.zeros(sbuf.shape[1:], F32)

    def all_pieces(s):
        return pltpu.make_async_copy(sbuf.at[s], xs_ref.at[pl.ds(0, SORT_ROWS)], sem.at[s])

    @pl.when(i >= 1)
    def _():
        all_pieces(slot).wait()

    pos = [p.astype(jnp.int32) for p in _local_positions(e_ref[...], T)]
    rio = lax.broadcasted_iota(jnp.int32, (SORT_ROWS, T), 0)
    pick = (rio == pos[0]) | (rio == pos[1]) | (rio == pos[2]) | (rio == pos[3])
    h = jnp.where(i < tiles_a, ha_ref[...], hb_ref[...])
    sbuf[slot] = _dot(pick.astype(BF16), h)

    _start_piece_copies(rows_ref, lambda r, g: pltpu.make_async_copy(
        sbuf.at[1 - slot, pl.ds(r, SUBLANES)], xs_ref.at[pl.ds(g, SUBLANES)], sem.at[1 - slot]))

    @pl.when(i == n_tiles)
    def _():
        all_pieces(1 - slot).wait()


def _dispatch(fill, piece_rows, eidx, h2_a, h2_b, n_rows, *, T):
    D = h2_a.shape[1]
    tiles_a, tiles_b = h2_a.shape[0] // T, h2_b.shape[0] // T
    n_tiles = tiles_a + tiles_b
    grid_spec = pltpu.PrefetchScalarGridSpec(
        num_scalar_prefetch=1,
        grid=(n_tiles + 1,),
        in_specs=[pl.BlockSpec((1, 1, PIECE_LANES), lambda i, fill: (i, 0, 0), memory_space=pltpu.SMEM),
                  pl.BlockSpec((TOP_K, T), lambda i, fill: (0, jnp.minimum(i, n_tiles - 1))),
                  pl.BlockSpec((T, D), lambda i, fill: (jnp.minimum(i, tiles_a - 1), 0)),
                  pl.BlockSpec((T, D), lambda i, fill: (jnp.clip(i - tiles_a, 0, tiles_b - 1), 0))],
        out_specs=pl.BlockSpec(memory_space=pl.ANY),
        scratch_shapes=[pltpu.VMEM((2, SORT_ROWS, D), F32), pltpu.VMEM((MOE_ROWS, D), F32),
                        pltpu.SemaphoreType.DMA((2,)), pltpu.SemaphoreType.DMA(())],
    )
    return pl.pallas_call(
        functools.partial(_dispatch_kernel, T=T, tiles_a=tiles_a),
        grid_spec=grid_spec,
        out_shape=jax.ShapeDtypeStruct((n_rows, D), F32),
        compiler_params=pltpu.CompilerParams(
            dimension_semantics=("arbitrary",), vmem_limit_bytes=VMEM_LIMIT),
        name="dispatch",
    )(fill, piece_rows, eidx, h2_a, h2_b)


def _moe_kernel(be_ref, nu_ref, xs_ref, wgu_ref, bgu_ref, wd_ref, bd_ref, y_ref, wgu_s, wd_s):
    i = pl.program_id(0)
    used = i < nu_ref[0]
    new_expert = jnp.logical_or(i == 0, be_ref[i] != be_ref[jnp.maximum(i - 1, 0)])

    @pl.when(jnp.logical_and(used, new_expert))
    def _():
        def cast_rows(r, c):
            rows = pl.ds(pl.multiple_of(r * LANES, LANES), LANES)
            wgu_s[rows, :] = wgu_ref[0, rows, :].astype(BF16)
            wd_s[rows, :] = wd_ref[0, rows, :].astype(BF16)
            return c
        lax.fori_loop(0, D_FF // LANES, cast_rows, 0)

    @pl.when(used)
    def _():
        x = xs_ref[...].astype(BF16)
        bgu = bgu_ref[0]
        acc = None
        for c in range(D_FF // FF_CHUNK):
            cs = slice(c * FF_CHUNK, (c + 1) * FF_CHUNK)
            ls = slice(D_FF + c * FF_CHUNK, D_FF + (c + 1) * FF_CHUNK)
            glu = jnp.minimum(_dot(x, wgu_s[:, cs]) + bgu[:, cs], SWIGLU_LIMIT)
            lin = jnp.clip(_dot(x, wgu_s[:, ls]) + bgu[:, ls], -SWIGLU_LIMIT, SWIGLU_LIMIT)
            act = glu * _sigmoid(SWIGLU_ALPHA * glu) * (lin + 1.0)
            part = _dot(act.astype(BF16), wd_s[cs, :])
            acc = part if acc is None else acc + part
        y_ref[...] = acc + bd_ref[0]

    @pl.when(jnp.logical_not(used))
    def _():
        def zero_rows(r, c):
            rows = pl.ds(pl.multiple_of(r * LANES, LANES), LANES)
            y_ref[rows, :] = jnp.zeros((LANES, y_ref.shape[1]), F32)
            return c
        lax.fori_loop(0, y_ref.shape[0] // LANES, zero_rows, 0)


def _moe(block_e, n_used, xs, w_gu, b_gu, w_down, b_down):
    P, D = xs.shape
    R = MOE_ROWS
    nb = P // R
    xs_map = lambda i, be, nu: (jnp.minimum(i, nu[0] - 1), 0)
    w_map = lambda i, be, nu: (be[i], 0, 0)
    grid_spec = pltpu.PrefetchScalarGridSpec(
        num_scalar_prefetch=2,
        grid=(nb,),
        in_specs=[
            pl.BlockSpec((R, D), xs_map),
            pl.BlockSpec((1, D, 2 * D_FF), w_map),
            pl.BlockSpec((1, 1, 2 * D_FF), w_map),
            pl.BlockSpec((1, D_FF, D), w_map),
            pl.BlockSpec((1, 1, D), w_map),
        ],
        out_specs=pl.BlockSpec((R, D), lambda i, be, nu: (i, 0)),
        scratch_shapes=[pltpu.VMEM((D, 2 * D_FF), BF16), pltpu.VMEM((D_FF, D), BF16)],
    )
    return pl.pallas_call(
        _moe_kernel,
        grid_spec=grid_spec,
        out_shape=jax.ShapeDtypeStruct((P, D), F32),
        compiler_params=pltpu.CompilerParams(
            dimension_semantics=("arbitrary",), vmem_limit_bytes=VMEM_LIMIT),
        name="moe_experts",
    )(block_e, n_used, xs, w_gu, b_gu, w_down, b_down)


def _combine_kernel(rows_ref, rowsn_ref, e_ref, gate_ref, x1_ref, mod_ref, g_ref, b_ref, y_hbm,
                    x2_ref, ybuf, sem, *, T):
    nj = pl.num_programs(1)
    step = pl.program_id(0) * nj + pl.program_id(1)
    n_steps = pl.num_programs(0) * nj
    slot = step % 2
    S = SORT_ROWS

    def piece_copy(s):
        return lambda r, g: pltpu.make_async_copy(
            y_hbm.at[pl.ds(g, SUBLANES)], ybuf.at[s, pl.ds(r, SUBLANES)], sem.at[s])

    def all_pieces(s):
        return pltpu.make_async_copy(y_hbm.at[pl.ds(0, S)], ybuf.at[s], sem.at[s])

    @pl.when(step == 0)
    def _():
        _start_piece_copies(rows_ref, piece_copy(0))

    all_pieces(slot).wait()
    _start_piece_copies(rowsn_ref, piece_copy(1 - slot))

    pos = _local_positions(e_ref[...], T)
    stacked = jnp.concatenate(pos + [gate_ref[...], jnp.zeros((LANES - 2 * TOP_K, T), F32)], axis=0).T
    cio = lax.broadcasted_iota(jnp.int32, (T, S), 1)
    w = jnp.zeros((T, S), F32)
    for k in range(TOP_K):
        w = w + jnp.where(cio == stacked[:, k:k + 1].astype(jnp.int32), stacked[:, TOP_K + k:TOP_K + k + 1], 0.0)
    y_sorted = ybuf[slot]
    w_hi = w.astype(BF16)
    w_lo = (w - w_hi.astype(F32)).astype(BF16)
    y_hi = y_sorted.astype(BF16)
    y_lo = (y_sorted - y_hi.astype(F32)).astype(BF16)
    ffn = _dot(w_hi, y_hi) + _dot(w_lo, y_hi) + _dot(w_hi, y_lo)
    bt, lt, d = x1_ref.shape
    g2 = mod_ref[...][:, 5:6, :]
    z = DEEPNORM_ALPHA * x1_ref[...] + g2 * ffn.reshape(bt, lt, d)
    x2_ref[...] = _ln(z) * g_ref[...] + b_ref[...]

    @pl.when(step == n_steps - 1)
    def _():
        all_pieces(1 - slot).wait()


def _combine(piece_rows, tile0, eidx, gates, x1, mod, ln_g, ln_b, y, *, T):
    B, L, D = x1.shape
    bt, lt = (1, T) if L >= T else (T // L, L)
    nj = L // lt
    n_tiles = (B // bt) * nj
    cur = lambda b, j: (0, b * nj + j)
    tab_cur = lambda b, j: (tile0 + b * nj + j, 0, 0)
    tab_nxt = lambda b, j: (tile0 + jnp.minimum(b * nj + j + 1, n_tiles - 1), 0, 0)
    return pl.pallas_call(
        functools.partial(_combine_kernel, T=T),
        grid=(B // bt, nj),
        in_specs=[
            pl.BlockSpec((1, 1, PIECE_LANES), tab_cur, memory_space=pltpu.SMEM),
            pl.BlockSpec((1, 1, PIECE_LANES), tab_nxt, memory_space=pltpu.SMEM),
            pl.BlockSpec((TOP_K, T), cur),
            pl.BlockSpec((TOP_K, T), cur),
            pl.BlockSpec((bt, lt, D), lambda b, j: (b, j, 0)),
            pl.BlockSpec((bt, 6, D), lambda b, j: (b, 0, 0)),
            pl.BlockSpec((1, D), lambda b, j: (0, 0)),
            pl.BlockSpec((1, D), lambda b, j: (0, 0)),
            pl.BlockSpec(memory_space=pl.ANY),
        ],
        out_specs=pl.BlockSpec((bt, lt, D), lambda b, j: (b, j, 0)),
        out_shape=jax.ShapeDtypeStruct((B, L, D), F32),
        scratch_shapes=[pltpu.VMEM((2, SORT_ROWS, D), F32), pltpu.SemaphoreType.DMA((2,))],
        compiler_params=pltpu.CompilerParams(
            dimension_semantics=("arbitrary", "arbitrary"), vmem_limit_bytes=VMEM_LIMIT),
        name="combine",
    )(piece_rows, piece_rows, eidx, gates, x1, mod, ln_g, ln_b, y)


def _routing_tables(cnt, n_blocks):
    R = MOE_ROWS
    n_tiles = cnt.shape[0]
    len8 = (cnt + SUBLANES - 1) // SUBLANES * SUBLANES
    total = jnp.sum(len8, axis=0)
    padded = (total + R - 1) // R * R
    pad_end = jnp.cumsum(padded)
    pad_start = pad_end - padded
    first_row = pad_start[None, :] + jnp.cumsum(len8, axis=0) - len8
    tile_end = jnp.cumsum(len8, axis=1)
    piece0 = jnp.arange(SORT_ROWS // SUBLANES, dtype=jnp.int32) * SUBLANES
    in_run = ((piece0[None, :, None] >= (tile_end - len8)[:, None, :])
              & (piece0[None, :, None] < tile_end[:, None, :]))
    shift = jnp.sum(jnp.where(in_run, (first_row - (tile_end - len8))[:, None, :], 0), axis=-1)
    spare = n_blocks * R + piece0
    rows = jnp.where(jnp.any(in_run, axis=-1), piece0[None, :] + shift, spare[None, :])
    rows = jnp.concatenate([spare[None, :], rows], axis=0)
    rows = jnp.pad(rows, ((0, 0), (0, PIECE_LANES - rows.shape[1])))
    piece_rows = rows.astype(jnp.int32).reshape(n_tiles + 1, 1, PIECE_LANES)
    blk0 = jnp.arange(n_blocks + -(-SORT_ROWS // R), dtype=jnp.int32) * R
    block_e = jnp.sum((blk0[:, None] >= pad_end[None, :]).astype(jnp.int32), axis=1)
    block_e = jnp.minimum(block_e, N_EXPERTS - 1)
    n_used = (pad_end[-1:] // R).astype(jnp.int32)
    fill = jnp.concatenate([pad_start + total, padded - total, n_used]).astype(jnp.int32)
    return piece_rows, block_e, n_used, fill


def _mix_group(x, mod, hist0, c0, n0, m0, mixer_w, *, T, pos0):
    B, L, _ = x.shape
    N = B * L
    m0p = jnp.pad(m0, ((0, 0), (0, LANES - N_HEADS))).reshape(B, 1, LANES)
    x1, h2, eidx, gates, hist, c_new, n_new, m_new = _mixer(
        x, mod, hist0, c0, n0, m0p, mixer_w, T=T, pos0=pos0)
    eidx = eidx.transpose(1, 0, 2).reshape(TOP_K, N)
    gates = gates.transpose(1, 0, 2).reshape(TOP_K, N)
    states = (hist, c_new, n_new, m_new.reshape(B, LANES)[:, :N_HEADS])
    return x1, h2.reshape(N, -1), eidx, gates, states


def kernel(x_prompt, x_sample, state_pool, state_mlstm_C, state_mlstm_n, state_mlstm_m, c_prompt, c_sample,
           w_ada, b_ada, w_in, b_if, w_group, pool_scale, mh_norm_g, w_proj_a, w_proj_b, w_out,
           ln1_g, ln1_b, w_router, b_router, w_gu, b_gu, w_down, b_down, ln2_g, ln2_b):
    assert w_ada.shape[0] == DEPTH == 1
    l = 0
    bp, bs = x_prompt.shape[0], x_sample.shape[0]
    dt = x_prompt.dtype
    row = lambda a: a.reshape(1, -1)

    wi = w_in[l]
    o0 = 0
    parts = []
    for width in (D_POOL, N_HEADS * HEAD_DK, N_HEADS * HEAD_DK, N_HEADS * HEAD_DV, N_HEADS * HEAD_DV):
        parts.append(wi[:, o0:o0 + width].astype(BF16))
        o0 += width
    w_if = jnp.pad(wi[:, o0:o0 + 2 * N_HEADS], ((0, 0), (0, LANES - 2 * N_HEADS))).astype(BF16)
    o0 += 2 * N_HEADS
    w_ga = wi[:, o0:o0 + D_MODEL].astype(BF16)
    w_gb = wi[:, o0 + D_MODEL:o0 + 2 * D_MODEL].astype(BF16)
    w_u, w_q, w_k, w_v, w_o = parts
    b_if_row = jnp.pad(b_if[l], (0, LANES - 2 * N_HEADS)).reshape(1, LANES)
    w_r = jnp.pad(w_router[l], ((0, 0), (0, LANES - N_EXPERTS))).astype(BF16)
    b_r = jnp.pad(b_router[l], (0, LANES - N_EXPERTS), constant_values=NEG_BIG).reshape(1, LANES)
    mixer_w = (w_u, w_q, w_k, w_v, w_o, w_ga, w_gb, w_if, b_if_row,
               w_group[l].astype(BF16), row(pool_scale[l]), row(mh_norm_g[l]),
               w_proj_a[l].astype(BF16), w_proj_b[l].astype(BF16), w_out[l].astype(BF16),
               row(ln1_g[l]), row(ln1_b[l]), w_r, b_r)
    moe_w = (w_gu[l], b_gu[l].reshape(N_EXPERTS, 1, 2 * D_FF),
             w_down[l], b_down[l].reshape(N_EXPERTS, 1, D_MODEL))

    mod = _ada(jnp.concatenate([c_prompt, c_sample], axis=0), w_ada[l], b_ada[l])
    mod = mod.reshape(bp + bs, 6, D_MODEL)

    zeros = lambda *s: jnp.zeros(s, dt)
    x1_p, h2_p, eidx_p, gates_p, st_p = _mix_group(
        x_prompt, mod[:bp], zeros(bp, POOL_HIST, D_POOL), zeros(bp, N_HEADS, HEAD_DK, HEAD_DV),
        zeros(bp, N_HEADS, HEAD_DK), zeros(bp, N_HEADS), mixer_w, T=min(256, x_prompt.shape[1]), pos0=0)
    x1_s, h2_s, eidx_s, gates_s, st_s = _mix_group(
        x_sample, mod[bp:], state_pool[l], state_mlstm_C[l], state_mlstm_n[l], state_mlstm_m[l],
        mixer_w, T=min(256, x_sample.shape[1]), pos0=PAST_LEN)

    n_p, n_s = h2_p.shape[0], h2_s.shape[0]
    eidx = jnp.concatenate([eidx_p, eidx_s], axis=1)
    T = ROW_TILE
    n_tiles = (n_p + n_s) // T
    cnt = _tile_counts(eidx, T=T)[:, 0, :N_EXPERTS]
    max_rows = (n_p + n_s) * TOP_K + n_tiles * N_EXPERTS * (SUBLANES - 1) + N_EXPERTS * (MOE_ROWS - 1)
    n_blocks = -(-max_rows // MOE_ROWS)
    piece_rows, block_e, n_used, fill = _routing_tables(cnt, n_blocks)
    n_rows = (n_blocks + -(-SORT_ROWS // MOE_ROWS)) * MOE_ROWS
    xs = _dispatch(fill, piece_rows, eidx, h2_p, h2_s, n_rows, T=T)
    y = _moe(block_e, n_used, xs, *moe_w)
    ln2 = (row(ln2_g[l]), row(ln2_b[l]))
    yp = _combine(piece_rows, 1, eidx_p, gates_p, x1_p, mod[:bp], *ln2, y, T=T)
    ys = _combine(piece_rows, 1 + n_p // T, eidx_s, gates_s, x1_s, mod[bp:], *ln2, y, T=T)
    st = lambda a: a[None]
    return (yp, ys) + tuple(st(a) for a in st_p) + tuple(st(a) for a in st_s)
```

```python
import functools

import jax
import jax.numpy as jnp
from jax import lax
from jax.experimental import pallas as pl
from jax.experimental.pallas import tpu as pltpu

D_MODEL = 1024
DEPTH = 1
D_POOL = 512
POOL_WINDOWS = (2, 4, 8, 16)
POOL_GROUP = 128
POOL_HIST = 15
N_HEADS = 4
HEAD_DK = 256
HEAD_DV = 256
N_EXPERTS = 32
TOP_K = 4
D_FF = 1024
SWIGLU_ALPHA = 1.702
SWIGLU_LIMIT = 7.0
LN_EPS = 1e-5
DEEPNORM_ALPHA = (2.0 * DEPTH) ** 0.25
PAST_LEN = 4096

LANES = 128
SUBLANES = 8
HIST_ROWS = POOL_HIST + 1
MOE_ROWS = 512
FF_CHUNK = 256
ROW_TILE = 256
SORT_ROWS = TOP_K * ROW_TILE + N_EXPERTS * SUBLANES
PIECE_LANES = -(-(SORT_ROWS // SUBLANES) // LANES) * LANES
VMEM_LIMIT = 56 * 1024 * 1024
NEG_BIG = -1e30

F32 = jnp.float32
BF16 = jnp.bfloat16


def _dot(a, b):
    return jnp.dot(a, b, preferred_element_type=F32)


def _ln(x):
    mu = jnp.mean(x, axis=-1, keepdims=True)
    xc = x - mu
    var = jnp.mean(xc * xc, axis=-1, keepdims=True)
    return xc * lax.rsqrt(var + LN_EPS)


def _sigmoid(x):
    return 1.0 / (1.0 + jnp.exp(-x))


def _log_sigmoid(x):
    return jnp.minimum(x, 0.0) - jnp.log(1.0 + jnp.exp(-jnp.abs(x)))


def _split3(x):
    hi = x.astype(BF16)
    r1 = x - hi.astype(F32)
    mid = r1.astype(BF16)
    lo = (r1 - mid.astype(F32)).astype(BF16)
    return hi, mid, lo


def _const_spec(shape):
    nd = len(shape)
    return pl.BlockSpec(shape, lambda *_: (0,) * nd, pipeline_mode=pl.Buffered(1))


def _ada_kernel(c_ref, w_ref, b_ref, o_ref):
    c = c_ref[...]
    s = c * _sigmoid(c)
    o_ref[...] = _dot(s.astype(BF16), w_ref[...].astype(BF16)) + b_ref[...]


def _ada(c, w_ada, b_ada):
    nb, d = c.shape
    n_out = w_ada.shape[1]
    blk = D_MODEL
    return pl.pallas_call(
        _ada_kernel,
        grid=(n_out // blk,),
        in_specs=[
            pl.BlockSpec((nb, d), lambda i: (0, 0)),
            pl.BlockSpec((d, blk), lambda i: (0, i)),
            pl.BlockSpec((1, blk), lambda i: (0, i)),
        ],
        out_specs=pl.BlockSpec((nb, blk), lambda i: (0, i)),
        out_shape=jax.ShapeDtypeStruct((nb, n_out), F32),
        name="ada_mod",
    )(c, w_ada, b_ada.reshape(1, n_out))


def _mixer_kernel(x_ref, mod_ref, hist0_ref, c0_ref, n0_ref, m0_ref,
                  wu_ref, wq_ref, wk_ref, wv_ref, wo_ref, wga_ref, wgb_ref, wif_ref, bif_ref,
                  wgrp_ref, pscale_ref, mhg_ref, wpa_ref, wpb_ref, wout_ref, ln1g_ref, ln1b_ref,
                  wr_ref, br_ref,
                  x1_ref, h2_ref, eidx_ref, gate_ref, hist_ref, c_ref, n_ref, m_ref,
                  c_s, n_s, m_s, pbuf, *, T, pos0):
    j = pl.program_id(1)
    nj = pl.num_programs(1)

    @pl.when(j == 0)
    def _():
        c_s[...] = c0_ref[0]
        n_s[...] = n0_ref[0]
        m_s[...] = m0_ref[0]
        pbuf[0:1, :] = jnp.zeros((1, D_POOL), F32)
        pbuf[1:HIST_ROWS, :] = hist0_ref[0]

    x = x_ref[0]
    mod = mod_ref[0]
    sh1, sc1, g1, sh2, sc2, g2 = [mod[i:i + 1] for i in range(6)]
    h1b = (_ln(x) * (1.0 + sc1) + sh1).astype(BF16)

    u = _dot(h1b, wu_ref[...])
    pbuf[HIST_ROWS:HIST_ROWS + T, :] = u
    pos = pos0 + j * T + lax.broadcasted_iota(jnp.int32, (T, 1), 0)
    mixed = []
    for g, w in enumerate(POOL_WINDOWS):
        cols = slice(g * POOL_GROUP, (g + 1) * POOL_GROUP)
        win = pbuf[HIST_ROWS:HIST_ROWS + T, cols]
        for i in range(1, w):
            win = win + pbuf[HIST_ROWS - i:HIST_ROWS - i + T, cols]
        cnt = jnp.minimum(w, pos + 1).astype(F32)
        pooled = win / cnt - u[:, cols]
        mixed.append(_dot(pooled.astype(BF16), wgrp_ref[g]))
    a = jnp.concatenate(mixed, axis=1) * pscale_ref[...]
    pbuf[0:HIST_ROWS, :] = pbuf[T:T + HIST_ROWS, :]

    gif = _dot(h1b, wif_ref[...]) + bif_ref[...]
    gif_t = gif.T
    lf_c = _log_sigmoid(gif)
    lf_r = _log_sigmoid(gif_t)
    row = lax.broadcasted_iota(jnp.int32, (T, T), 0)
    col = lax.broadcasted_iota(jnp.int32, (T, T), 1)
    causal = col <= row
    tri = causal.astype(BF16)
    tri_t = (row <= col).astype(BF16)
    b_c = sum(_dot(tri, p) for p in _split3(lf_c))
    b_r = sum(_dot(p, tri_t) for p in _split3(lf_r))

    q = _dot(h1b, wq_ref[...])
    k = _dot(h1b, wk_ref[...]) * (HEAD_DK ** -0.5)
    v = _dot(h1b, wv_ref[...])
    m_all = m_s[...]
    lane = lax.broadcasted_iota(jnp.int32, (1, LANES), 1)
    m_next = m_all
    heads = []
    for h in range(N_HEADS):
        sl = slice(h * HEAD_DK, (h + 1) * HEAD_DK)
        qh, kh = q[:, sl], k[:, sl]
        qb, kb, vb = qh.astype(BF16), kh.astype(BF16), v[:, sl].astype(BF16)
        bc = b_c[:, N_HEADS + h:N_HEADS + h + 1]
        br = b_r[N_HEADS + h:N_HEADS + h + 1, :]
        ig_c = gif[:, h:h + 1]
        ig_r = gif_t[h:h + 1, :]
        m_prev = m_all[:, h:h + 1]
        log_d = jnp.where(causal, bc - br + ig_r, -jnp.inf)
        log_inter = bc + m_prev
        m_t = jnp.maximum(log_inter, jnp.max(log_d, axis=1, keepdims=True))
        d = jnp.exp(log_d - m_t)
        s = lax.dot_general(qb, kb, (((1,), (1,)), ((), ())), preferred_element_type=F32) * d
        inter = jnp.exp(log_inter - m_t)
        c_h = c_s[h]
        n_h = n_s[h:h + 1, :]
        num = _dot(s.astype(BF16), vb) + inter * _dot(qb, c_h.astype(BF16))
        den = jnp.sum(s, axis=1, keepdims=True) + inter * jnp.sum(qh * n_h, axis=1, keepdims=True)
        heads.append(_ln(num / jnp.maximum(jnp.abs(den), jnp.exp(-m_t))))
        b_last = bc[T - 1:T, :]
        lw_c = b_last - bc + ig_c
        lw_r = b_last - br + ig_r
        m_new = jnp.maximum(b_last + m_prev, jnp.max(lw_r, axis=1, keepdims=True))
        decay = jnp.exp(b_last + m_prev - m_new)
        wk = kh * jnp.exp(lw_c - m_new)
        c_s[h] = decay * c_h + _dot(wk.T.astype(BF16), vb)
        n_s[h:h + 1, :] = decay * n_h + jnp.sum(wk, axis=0, keepdims=True)
        m_next = jnp.where(lane == h, m_new, m_next)
    m_s[...] = m_next

    hm = jnp.concatenate(heads, axis=1) * mhg_ref[...]
    hm = hm * _sigmoid(_dot(h1b, wo_ref[...]))
    merged = (_sigmoid(_dot(h1b, wga_ref[...])) * _dot(a.astype(BF16), wpa_ref[...])
              + _sigmoid(_dot(h1b, wgb_ref[...])) * _dot(hm.astype(BF16), wpb_ref[...]))
    mix = _dot(merged.astype(BF16), wout_ref[...])
    x1 = _ln(DEEPNORM_ALPHA * x + g1 * mix) * ln1g_ref[...] + ln1b_ref[...]
    h2 = _ln(x1) * (1.0 + sc2) + sh2
    x1_ref[0] = x1
    h2b = h2.astype(BF16)
    h2_ref[0] = h2b

    logits = _dot(h2b, wr_ref[...]) + br_ref[...]
    lt = logits.T[0:N_EXPERTS, :]
    eio = lax.broadcasted_iota(jnp.int32, (N_EXPERTS, T), 0)
    vals, idxs = [], []
    for _ in range(TOP_K):
        mx = jnp.max(lt, axis=0, keepdims=True)
        idx = jnp.min(jnp.where(lt == mx, eio, N_EXPERTS), axis=0, keepdims=True)
        vals.append(mx)
        idxs.append(idx)
        lt = jnp.where(eio == idx, -jnp.inf, lt)
    ex = [jnp.exp(vk - vals[0]) for vk in vals]
    tot = ex[0] + ex[1] + ex[2] + ex[3]
    eidx_ref[0] = jnp.concatenate(idxs, axis=0)
    gate_ref[0] = jnp.concatenate([e / tot for e in ex], axis=0)

    @pl.when(j == nj - 1)
    def _():
        hist_ref[0] = pbuf[1:HIST_ROWS, :]
        c_ref[0] = c_s[...]
        n_ref[0] = n_s[...]
        m_ref[0] = m_s[...]


def _mixer(x, mod, hist0, c0, n0, m0, wts, *, T, pos0):
    B, L, D = x.shape
    nj = L // T
    N = B * L
    kern = functools.partial(_mixer_kernel, T=T, pos0=pos0)
    seq_map = lambda b, j: (b, j, 0)
    st3 = lambda b, j: (b, 0, 0)
    st4 = lambda b, j: (b, 0, 0, 0)
    in_specs = [
        pl.BlockSpec((1, T, D), seq_map),
        pl.BlockSpec((1, 6, D), st3),
        pl.BlockSpec((1, POOL_HIST, D_POOL), st3),
        pl.BlockSpec((1, N_HEADS, HEAD_DK, HEAD_DV), st4),
        pl.BlockSpec((1, N_HEADS, HEAD_DK), st3),
        pl.BlockSpec((1, 1, LANES), st3),
    ] + [_const_spec(w.shape) for w in wts]
    out_specs = [
        pl.BlockSpec((1, T, D), seq_map),
        pl.BlockSpec((1, T, D), seq_map),
        pl.BlockSpec((1, TOP_K, T), lambda b, j: (b * nj + j, 0, 0)),
        pl.BlockSpec((1, TOP_K, T), lambda b, j: (b * nj + j, 0, 0)),
        pl.BlockSpec((1, POOL_HIST, D_POOL), st3),
        pl.BlockSpec((1, N_HEADS, HEAD_DK, HEAD_DV), st4),
        pl.BlockSpec((1, N_HEADS, HEAD_DK), st3),
        pl.BlockSpec((1, 1, LANES), st3),
    ]
    out_shape = [
        jax.ShapeDtypeStruct((B, L, D), F32),
        jax.ShapeDtypeStruct((B, L, D), BF16),
        jax.ShapeDtypeStruct((N // T, TOP_K, T), jnp.int32),
        jax.ShapeDtypeStruct((N // T, TOP_K, T), F32),
        jax.ShapeDtypeStruct((B, POOL_HIST, D_POOL), F32),
        jax.ShapeDtypeStruct((B, N_HEADS, HEAD_DK, HEAD_DV), F32),
        jax.ShapeDtypeStruct((B, N_HEADS, HEAD_DK), F32),
        jax.ShapeDtypeStruct((B, 1, LANES), F32),
    ]
    scratch = [
        pltpu.VMEM((N_HEADS, HEAD_DK, HEAD_DV), F32),
        pltpu.VMEM((N_HEADS, HEAD_DK), F32),
        pltpu.VMEM((1, LANES), F32),
        pltpu.VMEM((T + HIST_ROWS, D_POOL), F32),
    ]
    return pl.pallas_call(
        kern,
        grid=(B, nj),
        in_specs=in_specs,
        out_specs=out_specs,
        out_shape=out_shape,
        scratch_shapes=scratch,
        compiler_params=pltpu.CompilerParams(
            dimension_semantics=("arbitrary", "arbitrary"), vmem_limit_bytes=VMEM_LIMIT),
        name="mixer",
    )(x, mod, hist0, c0, n0, m0, *wts)


def _pad8(x):
    return jnp.floor((x + (SUBLANES - 1)) * (1.0 / SUBLANES)) * SUBLANES


def _expert_counts_row(oh, T):
    ones = jnp.ones((SUBLANES, T), BF16)
    return lax.dot_general(ones, oh.astype(BF16), (((1,), (1,)), ((), ())), preferred_element_type=F32)[0:1]


def _local_positions(e, T):
    eio = lax.broadcasted_iota(jnp.int32, (N_EXPERTS, T), 0)
    hits = [eio == e[k:k + 1, :] for k in range(TOP_K)]
    oh = sum(h.astype(F32) for h in hits)
    row = lax.broadcasted_iota(jnp.int32, (T, T), 0)
    col = lax.broadcasted_iota(jnp.int32, (T, T), 1)
    before = _dot(oh.astype(BF16), (row < col).astype(BF16))
    len8 = _pad8(_expert_counts_row(oh, T))
    er = lax.broadcasted_iota(jnp.int32, (N_EXPERTS, N_EXPERTS), 0)
    ec = lax.broadcasted_iota(jnp.int32, (N_EXPERTS, N_EXPERTS), 1)
    start = jnp.sum(jnp.where(ec < er, len8, 0.0), axis=1, keepdims=True)
    base = before + start
    return [jnp.sum(jnp.where(h, base, 0.0), axis=0, keepdims=True) for h in hits]


def _count_kernel(e_ref, cnt_ref, *, T, tiles):
    eio = lax.broadcasted_iota(jnp.int32, (LANES, T), 0)
    for b in range(tiles):
        e = e_ref[:, b * T:(b + 1) * T]
        oh = sum((eio == e[k:k + 1, :]).astype(F32) for k in range(TOP_K))
        cnt_ref[b] = _expert_counts_row(oh, T).astype(jnp.int32)


def _tile_counts(eidx, *, T):
    K, N = eidx.shape
    n_tiles = N // T
    tiles = next(t for t in (8, 4, 2, 1) if n_tiles % t == 0)
    return pl.pallas_call(
        functools.partial(_count_kernel, T=T, tiles=tiles),
        grid=(n_tiles // tiles,),
        in_specs=[pl.BlockSpec((K, tiles * T), lambda i: (0, i))],
        out_specs=pl.BlockSpec((tiles, 1, LANES), lambda i: (i, 0, 0)),
        out_shape=jax.ShapeDtypeStruct((n_tiles, 1, LANES), jnp.int32),
        name="tile_counts",
    )(eidx)


def _start_piece_copies(rows_ref, make_copy):
    for p in range(SORT_ROWS // SUBLANES):
        make_copy(p * SUBLANES, pl.multiple_of(rows_ref[0, 0, p], SUBLANES)).start()


def _dispatch_kernel(fill_ref, rows_ref, e_ref, ha_ref, hb_ref, xs_ref, sbuf, zbuf, sem, zsem, *, T, tiles_a):
    i = pl.program_id(0)
    n_tiles = pl.num_programs(0) - 1
    slot = i % 2

    def zero_fill(finish):
        def go(off, size):
            cp = pltpu.make_async_copy(zbuf.at[pl.ds(0, size)], xs_ref.at[pl.ds(off, size)], zsem)
            cp.wait() if finish else cp.start()

        for e in range(N_EXPERTS):
            off, pad = pl.multiple_of(fill_ref[e], SUBLANES), fill_ref[N_EXPERTS + e]
            for bit in range(SUBLANES.bit_length() - 1, MOE_ROWS.bit_length() - 1):
                size = 1 << bit
                pl.when((pad & size) != 0)(functools.partial(go, off, size))
                off = pl.multiple_of(off + (pad & size), SUBLANES)

        def tail(blk, c):
            go(pl.multiple_of(blk * MOE_ROWS, MOE_ROWS), MOE_ROWS)
            return c
        lax.fori_loop(fill_ref[2 * N_EXPERTS], xs_ref.shape[0] // MOE_ROWS, tail, 0)

    @pl.when(i == 0)
    def _():
        zbuf[...] = jnp.zeros_like(zbuf)
        zero_fill(False)
        zero_fill(True)
        sbuf[1] = jnp.zeros(sbuf.shape[1:], F32)

    def all_pieces(s):
        return pltpu.make_async_copy(sbuf.at[s], xs_ref.at[pl.ds(0, SORT_ROWS)], sem.at[s])

    @pl.when(i >= 1)
    def _():
        all_pieces(slot).wait()

    _start_piece_copies(rows_ref, lambda r, g: pltpu.make_async_copy(
        sbuf.at[1 - slot, pl.ds(r, SUBLANES)], xs_ref.at[pl.ds(g, SUBLANES)], sem.at[1 - slot]))

    pos = [p.astype(jnp.int32) for p in _local_positions(e_ref[...], T)]
    rio = lax.broadcasted_iota(jnp.int32, (SORT_ROWS, T), 0)
    pick = (rio == pos[0]) | (rio == pos[1]) | (rio == pos[2]) | (rio == pos[3])
    h = jnp.where(i < tiles_a, ha_ref[...], hb_ref[...])
    sbuf[slot] = _dot(pick.astype(BF16), h)

    @pl.when(i == n_tiles)
    def _():
        all_pieces(1 - slot).wait()


def _dispatch(fill, piece_rows, eidx, h2_a, h2_b, n_rows, *, T):
    D = h2_a.shape[1]
    tiles_a, tiles_b = h2_a.shape[0] // T, h2_b.shape[0] // T
    n_tiles = tiles_a + tiles_b
    grid_spec = pltpu.PrefetchScalarGridSpec(
        num_scalar_prefetch=1,
        grid=(n_tiles + 1,),
        in_specs=[pl.BlockSpec((1, 1, PIECE_LANES), lambda i, fill: (i, 0, 0), memory_space=pltpu.SMEM),
                  pl.BlockSpec((TOP_K, T), lambda i, fill: (0, jnp.minimum(i, n_tiles - 1))),
                  pl.BlockSpec((T, D), lambda i, fill: (jnp.minimum(i, tiles_a - 1), 0)),
                  pl.BlockSpec((T, D), lambda i, fill: (jnp.clip(i - tiles_a, 0, tiles_b - 1), 0))],
        out_specs=pl.BlockSpec(memory_space=pl.ANY),
        scratch_shapes=[pltpu.VMEM((2, SORT_ROWS, D), F32), pltpu.VMEM((MOE_ROWS, D), F32),
                        pltpu.SemaphoreType.DMA((2,)), pltpu.SemaphoreType.DMA(())],
    )
    return pl.pallas_call(
        functools.partial(_dispatch_kernel, T=T, tiles_a=tiles_a),
        grid_spec=grid_spec,
        out_shape=jax.ShapeDtypeStruct((n_rows, D), F32),
        compiler_params=pltpu.CompilerParams(
            dimension_semantics=("arbitrary",), vmem_limit_bytes=VMEM_LIMIT),
        name="dispatch",
    )(fill, piece_rows, eidx, h2_a, h2_b)


def _moe_kernel(be_ref, nu_ref, xs_ref, wgu_ref, bgu_ref, wd_ref, bd_ref, y_ref, wgu_s, wd_s):
    i = pl.program_id(0)
    used = i < nu_ref[0]
    new_expert = jnp.logical_or(i == 0, be_ref[i] != be_ref[jnp.maximum(i - 1, 0)])

    @pl.when(jnp.logical_and(used, new_expert))
    def _():
        def cast_rows(r, c):
            rows = pl.ds(pl.multiple_of(r * LANES, LANES), LANES)
            wgu_s[rows, :] = wgu_ref[0, rows, :].astype(BF16)
            wd_s[rows, :] = wd_ref[0, rows, :].astype(BF16)
            return c
        lax.fori_loop(0, D_FF // LANES, cast_rows, 0)

    @pl.when(used)
    def _():
        x = xs_ref[...].astype(BF16)
        bgu = bgu_ref[0]
        acc = None
        for c in range(D_FF // FF_CHUNK):
            cs = slice(c * FF_CHUNK, (c + 1) * FF_CHUNK)
            ls = slice(D_FF + c * FF_CHUNK, D_FF + (c + 1) * FF_CHUNK)
            glu = jnp.minimum(_dot(x, wgu_s[:, cs]) + bgu[:, cs], SWIGLU_LIMIT)
            lin = jnp.clip(_dot(x, wgu_s[:, ls]) + bgu[:, ls], -SWIGLU_LIMIT, SWIGLU_LIMIT)
            act = glu * _sigmoid(SWIGLU_ALPHA * glu) * (lin + 1.0)
            part = _dot(act.astype(BF16), wd_s[cs, :])
            acc = part if acc is None else acc + part
        y_ref[...] = acc + bd_ref[0]

    @pl.when(jnp.logical_not(used))
    def _():
        def zero_rows(r, c):
            rows = pl.ds(pl.multiple_of(r * LANES, LANES), LANES)
            y_ref[rows, :] = jnp.zeros((LANES, y_ref.shape[1]), F32)
            return c
        lax.fori_loop(0, y_ref.shape[0] // LANES, zero_rows, 0)


def _moe(block_e, n_used, xs, w_gu, b_gu, w_down, b_down):
    P, D = xs.shape
    R = MOE_ROWS
    nb = P // R
    xs_map = lambda i, be, nu: (jnp.minimum(i, nu[0] - 1), 0)
    w_map = lambda i, be, nu: (be[i], 0, 0)
    grid_spec = pltpu.PrefetchScalarGridSpec(
        num_scalar_prefetch=2,
        grid=(nb,),
        in_specs=[
            pl.BlockSpec((R, D), xs_map),
            pl.BlockSpec((1, D, 2 * D_FF), w_map),
            pl.BlockSpec((1, 1, 2 * D_FF), w_map),
            pl.BlockSpec((1, D_FF, D), w_map),
            pl.BlockSpec((1, 1, D), w_map),
        ],
        out_specs=pl.BlockSpec((R, D), lambda i, be, nu: (i, 0)),
        scratch_shapes=[pltpu.VMEM((D, 2 * D_FF), BF16), pltpu.VMEM((D_FF, D), BF16)],
    )
    return pl.pallas_call(
        _moe_kernel,
        grid_spec=grid_spec,
        out_shape=jax.ShapeDtypeStruct((P, D), F32),
        compiler_params=pltpu.CompilerParams(
            dimension_semantics=("arbitrary",), vmem_limit_bytes=VMEM_LIMIT),
        name="moe_experts",
    )(block_e, n_used, xs, w_gu, b_gu, w_down, b_down)


def _combine_kernel(rows_ref, rowsn_ref, e_ref, gate_ref, x1_ref, mod_ref, g_ref, b_ref, y_hbm,
                    x2_ref, ybuf, sem, *, T):
    nj = pl.num_programs(1)
    step = pl.program_id(0) * nj + pl.program_id(1)
    n_steps = pl.num_programs(0) * nj
    slot = step % 2
    S = SORT_ROWS

    def piece_copy(s):
        return lambda r, g: pltpu.make_async_copy(
            y_hbm.at[pl.ds(g, SUBLANES)], ybuf.at[s, pl.ds(r, SUBLANES)], sem.at[s])

    def all_pieces(s):
        return pltpu.make_async_copy(y_hbm.at[pl.ds(0, S)], ybuf.at[s], sem.at[s])

    @pl.when(step == 0)
    def _():
        _start_piece_copies(rows_ref, piece_copy(0))

    all_pieces(slot).wait()
    _start_piece_copies(rowsn_ref, piece_copy(1 - slot))

    pos = _local_positions(e_ref[...], T)
    stacked = jnp.concatenate(pos + [gate_ref[...], jnp.zeros((LANES - 2 * TOP_K, T), F32)], axis=0).T
    pos_c = [stacked[:, k:k + 1].astype(jnp.int32) for k in range(TOP_K)]
    gate_c = [stacked[:, TOP_K + k:TOP_K + k + 1] for k in range(TOP_K)]
    cio = lax.broadcasted_iota(jnp.int32, (T, S), 1)
    w = jnp.where(cio == pos_c[0], gate_c[0], 0.0)
    for k in range(1, TOP_K):
        w = w + jnp.where(cio == pos_c[k], gate_c[k], 0.0)
    w_hi = w.astype(BF16)
    w_lo = (w - w_hi.astype(F32)).astype(BF16)
    y_b = ybuf[slot].astype(BF16)
    ffn = _dot(w_hi, y_b) + _dot(w_lo, y_b)
    bt, lt, d = x1_ref.shape
    g2 = mod_ref[...][:, 5:6, :]
    z = DEEPNORM_ALPHA * x1_ref[...] + g2 * ffn.reshape(bt, lt, d)
    x2_ref[...] = _ln(z) * g_ref[...] + b_ref[...]

    @pl.when(step == n_steps - 1)
    def _():
        all_pieces(1 - slot).wait()


def _combine(piece_rows, tile0, eidx, gates, x1, mod, ln_g, ln_b, y, *, T):
    B, L, D = x1.shape
    bt, lt = (1, T) if L >= T else (T // L, L)
    nj = L // lt
    n_tiles = (B // bt) * nj
    cur = lambda b, j: (0, b * nj + j)
    tab_cur = lambda b, j: (tile0 + b * nj + j, 0, 0)
    tab_nxt = lambda b, j: (tile0 + jnp.minimum(b * nj + j + 1, n_tiles - 1), 0, 0)
    return pl.pallas_call(
        functools.partial(_combine_kernel, T=T),
        grid=(B // bt, nj),
        in_specs=[
            pl.BlockSpec((1, 1, PIECE_LANES), tab_cur, memory_space=pltpu.SMEM),
            pl.BlockSpec((1, 1, PIECE_LANES), tab_nxt, memory_space=pltpu.SMEM),
            pl.BlockSpec((TOP_K, T), cur),
            pl.BlockSpec((TOP_K, T), cur),
            pl.BlockSpec((bt, lt, D), lambda b, j: (b, j, 0)),
            pl.BlockSpec((bt, 6, D), lambda b, j: (b, 0, 0)),
            pl.BlockSpec((1, D), lambda b, j: (0, 0)),
            pl.BlockSpec((1, D), lambda b, j: (0, 0)),
            pl.BlockSpec(memory_space=pl.ANY),
        ],
        out_specs=pl.BlockSpec((bt, lt, D), lambda b, j: (b, j, 0)),
        out_shape=jax.ShapeDtypeStruct((B, L, D), F32),
        scratch_shapes=[pltpu.VMEM((2, SORT_ROWS, D), F32), pltpu.SemaphoreType.DMA((2,))],
        compiler_params=pltpu.CompilerParams(
            dimension_semantics=("arbitrary", "arbitrary"), vmem_limit_bytes=VMEM_LIMIT),
        name="combine",
    )(piece_rows, piece_rows, eidx, gates, x1, mod, ln_g, ln_b, y)


def _routing_tables(cnt, n_blocks):
    R = MOE_ROWS
    n_tiles = cnt.shape[0]
    len8 = (cnt + SUBLANES - 1) // SUBLANES * SUBLANES
    total = jnp.sum(len8, axis=0)
    padded = (total + R - 1) // R * R
    pad_end = jnp.cumsum(padded)
    pad_start = pad_end - padded
    first_row = pad_start[None, :] + jnp.cumsum(len8, axis=0) - len8
    tile_end = jnp.cumsum(len8, axis=1)
    piece0 = jnp.arange(SORT_ROWS // SUBLANES, dtype=jnp.int32) * SUBLANES
    in_run = ((piece0[None, :, None] >= (tile_end - len8)[:, None, :])
              & (piece0[None, :, None] < tile_end[:, None, :]))
    shift = jnp.sum(jnp.where(in_run, (first_row - (tile_end - len8))[:, None, :], 0), axis=-1)
    spare = n_blocks * R + piece0
    rows = jnp.where(jnp.any(in_run, axis=-1), piece0[None, :] + shift, spare[None, :])
    rows = jnp.concatenate([spare[None, :], rows], axis=0)
    rows = jnp.pad(rows, ((0, 0), (0, PIECE_LANES - rows.shape[1])))
    piece_rows = rows.astype(jnp.int32).reshape(n_tiles + 1, 1, PIECE_LANES)
    blk0 = jnp.arange(n_blocks + -(-SORT_ROWS // R), dtype=jnp.int32) * R
    block_e = jnp.sum((blk0[:, None] >= pad_end[None, :]).astype(jnp.int32), axis=1)
    block_e = jnp.minimum(block_e, N_EXPERTS - 1)
    n_used = (pad_end[-1:] // R).astype(jnp.int32)
    fill = jnp.concatenate([pad_start + total, padded - total, n_used]).astype(jnp.int32)
    return piece_rows, block_e, n_used, fill


def _mix_group(x, mod, hist0, c0, n0, m0, mixer_w, *, T, pos0):
    B, L, _ = x.shape
    N = B * L
    m0p = jnp.pad(m0, ((0, 0), (0, LANES - N_HEADS))).reshape(B, 1, LANES)
    x1, h2, eidx, gates, hist, c_new, n_new, m_new = _mixer(
        x, mod, hist0, c0, n0, m0p, mixer_w, T=T, pos0=pos0)
    eidx = eidx.transpose(1, 0, 2).reshape(TOP_K, N)
    gates = gates.transpose(1, 0, 2).reshape(TOP_K, N)
    states = (hist, c_new, n_new, m_new.reshape(B, LANES)[:, :N_HEADS])
    return x1, h2.reshape(N, -1), eidx, gates, states


def kernel(x_prompt, x_sample, state_pool, state_mlstm_C, state_mlstm_n, state_mlstm_m, c_prompt, c_sample,
           w_ada, b_ada, w_in, b_if, w_group, pool_scale, mh_norm_g, w_proj_a, w_proj_b, w_out,
           ln1_g, ln1_b, w_router, b_router, w_gu, b_gu, w_down, b_down, ln2_g, ln2_b):
    assert w_ada.shape[0] == DEPTH == 1
    l = 0
    bp, bs = x_prompt.shape[0], x_sample.shape[0]
    dt = x_prompt.dtype
    row = lambda a: a.reshape(1, -1)

    wi = w_in[l]
    o0 = 0
    parts = []
    for width in (D_POOL, N_HEADS * HEAD_DK, N_HEADS * HEAD_DK, N_HEADS * HEAD_DV, N_HEADS * HEAD_DV):
        parts.append(wi[:, o0:o0 + width].astype(BF16))
        o0 += width
    w_if = jnp.pad(wi[:, o0:o0 + 2 * N_HEADS], ((0, 0), (0, LANES - 2 * N_HEADS))).astype(BF16)
    o0 += 2 * N_HEADS
    w_ga = wi[:, o0:o0 + D_MODEL].astype(BF16)
    w_gb = wi[:, o0 + D_MODEL:o0 + 2 * D_MODEL].astype(BF16)
    w_u, w_q, w_k, w_v, w_o = parts
    b_if_row = jnp.pad(b_if[l], (0, LANES - 2 * N_HEADS)).reshape(1, LANES)
    w_r = jnp.pad(w_router[l], ((0, 0), (0, LANES - N_EXPERTS))).astype(BF16)
    b_r = jnp.pad(b_router[l], (0, LANES - N_EXPERTS), constant_values=NEG_BIG).reshape(1, LANES)
    mixer_w = (w_u, w_q, w_k, w_v, w_o, w_ga, w_gb, w_if, b_if_row,
               w_group[l].astype(BF16), row(pool_scale[l]), row(mh_norm_g[l]),
               w_proj_a[l].astype(BF16), w_proj_b[l].astype(BF16), w_out[l].astype(BF16),
               row(ln1_g[l]), row(ln1_b[l]), w_r, b_r)
    moe_w = (w_gu[l], b_gu[l].reshape(N_EXPERTS, 1, 2 * D_FF),
             w_down[l], b_down[l].reshape(N_EXPERTS, 1, D_MODEL))

    mod = _ada(jnp.concatenate([c_prompt, c_sample], axis=0), w_ada[l], b_ada[l])
    mod = mod.reshape(bp + bs, 6, D_MODEL)

    zeros = lambda *s: jnp.zeros(s, dt)
    x1_p, h2_p, eidx_p, gates_p, st_p = _mix_group(
        x_prompt, mod[:bp], zeros(bp, POOL_HIST, D_POOL), zeros(bp, N_HEADS, HEAD_DK, HEAD_DV),
        zeros(bp, N_HEADS, HEAD_DK), zeros(bp, N_HEADS), mixer_w, T=min(256, x_prompt.shape[1]), pos0=0)
    x1_s, h2_s, eidx_s, gates_s, st_s = _mix_group(
        x_sample, mod[bp:], state_pool[l], state_mlstm_C[l], state_mlstm_n[l], state_mlstm_m[l],
        mixer_w, T=min(256, x_sample.shape[1]), pos0=PAST_LEN)

    n_p, n_s = h2_p.shape[0], h2_s.shape[0]
    eidx = jnp.concatenate([eidx_p, eidx_s], axis=1)
    T = ROW_TILE
    n_tiles = (n_p + n_s) // T
    cnt = _tile_counts(eidx, T=T)[:, 0, :N_EXPERTS]
    max_rows = (n_p + n_s) * TOP_K + n_tiles * N_EXPERTS * (SUBLANES - 1) + N_EXPERTS * (MOE_ROWS - 1)
    n_blocks = -(-max_rows // MOE_ROWS)
    piece_rows, block_e, n_used, fill = _routing_tables(cnt, n_blocks)
    n_rows = (n_blocks + -(-SORT_ROWS // MOE_ROWS)) * MOE_ROWS
    xs = _dispatch(fill, piece_rows, eidx, h2_p, h2_s, n_rows, T=T)
    y = _moe(block_e, n_used, xs, *moe_w)
    ln2 = (row(ln2_g[l]), row(ln2_b[l]))
    yp = _combine(piece_rows, 1, eidx_p, gates_p, x1_p, mod[:bp], *ln2, y, T=T)
    ys = _combine(piece_rows, 1 + n_p // T, eidx_s, gates_s, x1_s, mod[bp:], *ln2, y, T=T)
    st = lambda a: a[None]
    return (yp, ys) + tuple(st(a) for a in st_p) + tuple(st(a) for a in st_s)
```

```python
import functools

import jax
import jax.numpy as jnp
from jax import lax
from jax.experimental import pallas as pl
from jax.experimental.pallas import tpu as pltpu

D_MODEL = 1024
DEPTH = 1
D_POOL = 512
POOL_WINDOWS = (2, 4, 8, 16)
POOL_GROUP = 128
POOL_HIST = 15
N_HEADS = 4
HEAD_DK = 256
HEAD_DV = 256
N_EXPERTS = 32
TOP_K = 4
D_FF = 1024
SWIGLU_ALPHA = 1.702
SWIGLU_LIMIT = 7.0
LN_EPS = 1e-5
DEEPNORM_ALPHA = (2.0 * DEPTH) ** 0.25
PAST_LEN = 4096

LANES = 128
SUBLANES = 8
HIST_ROWS = POOL_HIST + 1
MOE_ROWS = 512
FF_CHUNK = 512
ROW_TILE = 256
SORT_ROWS = TOP_K * ROW_TILE + N_EXPERTS * SUBLANES
PIECE_LANES = -(-(SORT_ROWS // SUBLANES) // LANES) * LANES
VMEM_LIMIT = 56 * 1024 * 1024
NEG_BIG = -1e30

F32 = jnp.float32
BF16 = jnp.bfloat16


def _dot(a, b):
    return jnp.dot(a, b, preferred_element_type=F32)


def _ln(x):
    mu = jnp.mean(x, axis=-1, keepdims=True)
    xc = x - mu
    var = jnp.mean(xc * xc, axis=-1, keepdims=True)
    return xc * lax.rsqrt(var + LN_EPS)


def _sigmoid(x):
    return 0.5 * jnp.tanh(0.5 * x) + 0.5


def _log_sigmoid(x):
    return jnp.minimum(x, 0.0) - jnp.log(1.0 + jnp.exp(-jnp.abs(x)))


def _split3(x):
    hi = x.astype(BF16)
    r1 = x - hi.astype(F32)
    mid = r1.astype(BF16)
    lo = (r1 - mid.astype(F32)).astype(BF16)
    return hi, mid, lo


def _const_spec(shape):
    nd = len(shape)
    return pl.BlockSpec(shape, lambda *_: (0,) * nd, pipeline_mode=pl.Buffered(1))


def _ada_kernel(c_ref, w_ref, b_ref, o_ref):
    c = c_ref[...]
    s = c * _sigmoid(c)
    o_ref[...] = _dot(s.astype(BF16), w_ref[...].astype(BF16)) + b_ref[...]


def _ada(c, w_ada, b_ada):
    nb, d = c.shape
    n_out = w_ada.shape[1]
    blk = D_MODEL
    return pl.pallas_call(
        _ada_kernel,
        grid=(n_out // blk,),
        in_specs=[
            pl.BlockSpec((nb, d), lambda i: (0, 0)),
            pl.BlockSpec((d, blk), lambda i: (0, i)),
            pl.BlockSpec((1, blk), lambda i: (0, i)),
        ],
        out_specs=pl.BlockSpec((nb, blk), lambda i: (0, i)),
        out_shape=jax.ShapeDtypeStruct((nb, n_out), F32),
        name="ada_mod",
    )(c, w_ada, b_ada.reshape(1, n_out))


def _mixer_kernel(x_ref, mod_ref, hist0_ref, c0_ref, n0_ref, m0_ref,
                  wu_ref, wq_ref, wk_ref, wv_ref, wo_ref, wga_ref, wgb_ref, wif_ref, bif_ref,
                  wgrp_ref, pscale_ref, mhg_ref, wpa_ref, wpb_ref, wout_ref, ln1g_ref, ln1b_ref,
                  wr_ref, br_ref,
                  x1_ref, h2_ref, eidx_ref, gate_ref, hist_ref, c_ref, n_ref, m_ref,
                  pbuf, c_s, n_s, m_s, *, T, pos0):
    j = pl.program_id(1)
    nj = pl.num_programs(1)
    nb = x_ref.shape[0]
    R = nb * T

    @pl.when(j == 0)
    def _():
        c_s[...] = c0_ref[...]
        n_s[...] = n0_ref[...]
        m_s[...] = m0_ref[...]
        pbuf[:, 0:1, :] = jnp.zeros((nb, 1, D_POOL), F32)
        pbuf[:, 1:HIST_ROWS, :] = hist0_ref[...]

    x = x_ref[...].reshape(R, D_MODEL)

    def mod_rows(i):
        if nb == 1:
            return mod_ref[0][i:i + 1]
        return jnp.concatenate([jnp.broadcast_to(mod_ref[s][i:i + 1], (T, D_MODEL)) for s in range(nb)], axis=0)

    sh1, sc1, g1, sh2, sc2, g2 = [mod_rows(i) for i in range(6)]
    h1b = (_ln(x) * (1.0 + sc1) + sh1).astype(BF16)

    u = _dot(h1b, wu_ref[...])
    pos = pos0 + j * T + lax.broadcasted_iota(jnp.int32, (T, 1), 0)
    for s in range(nb):
        pbuf[s, HIST_ROWS:HIST_ROWS + T, :] = u[s * T:(s + 1) * T]
    mixed = []
    for g, w in enumerate(POOL_WINDOWS):
        cols = slice(g * POOL_GROUP, (g + 1) * POOL_GROUP)
        cnt = jnp.minimum(w, pos + 1).astype(F32)
        wins = []
        for s in range(nb):
            win = pbuf[s, HIST_ROWS:HIST_ROWS + T, cols]
            for i in range(1, w):
                win = win + pbuf[s, HIST_ROWS - i:HIST_ROWS - i + T, cols]
            wins.append(win / cnt)
        pooled = jnp.concatenate(wins, axis=0) - u[:, cols]
        mixed.append(_dot(pooled.astype(BF16), wgrp_ref[g]))
    a = jnp.concatenate(mixed, axis=1) * pscale_ref[...]
    for s in range(nb):
        pbuf[s, 0:HIST_ROWS, :] = pbuf[s, T:T + HIST_ROWS, :]

    gif = _dot(h1b, wif_ref[...]) + bif_ref[...]
    lf = _log_sigmoid(gif)
    row = lax.broadcasted_iota(jnp.int32, (T, T), 0)
    col = lax.broadcasted_iota(jnp.int32, (T, T), 1)
    causal = col <= row
    tri = causal.astype(BF16)
    tri_t = (row <= col).astype(BF16)
    gif_ts = [gif[s * T:(s + 1) * T].T for s in range(nb)]
    b_cs = [sum(_dot(tri, p) for p in _split3(lf[s * T:(s + 1) * T])) for s in range(nb)]
    b_rs = [sum(_dot(p, tri_t) for p in _split3(_log_sigmoid(g))) for g in gif_ts]
    q = _dot(h1b, wq_ref[...])
    k = _dot(h1b, wk_ref[...]) * (HEAD_DK ** -0.5)
    v = _dot(h1b, wv_ref[...])
    lane = lax.broadcasted_iota(jnp.int32, (1, LANES), 1)
    seq_heads = []
    for s in range(nb):
        rs = slice(s * T, (s + 1) * T)
        gif_s, gif_t, b_c, b_r = gif[rs], gif_ts[s], b_cs[s], b_rs[s]
        m_all = m_s[s]
        m_next = m_all
        heads = []
        for h in range(N_HEADS):
            sl = slice(h * HEAD_DK, (h + 1) * HEAD_DK)
            qh, kh = q[rs, sl], k[rs, sl]
            qb, kb, vb = qh.astype(BF16), kh.astype(BF16), v[rs, sl].astype(BF16)
            bc = b_c[:, N_HEADS + h:N_HEADS + h + 1]
            br = b_r[N_HEADS + h:N_HEADS + h + 1, :]
            ig_c = gif_s[:, h:h + 1]
            ig_r = gif_t[h:h + 1, :]
            m_prev = m_all[:, h:h + 1]
            log_d = jnp.where(causal, bc - br + ig_r, -jnp.inf)
            log_inter = bc + m_prev
            m_t = jnp.maximum(log_inter, jnp.max(log_d, axis=1, keepdims=True))
            d = jnp.exp(log_d - m_t)
            sc = lax.dot_general(qb, kb, (((1,), (1,)), ((), ())), preferred_element_type=F32) * d
            inter = jnp.exp(log_inter - m_t)
            c_h = c_s[s, h]
            n_h = n_s[s, h:h + 1, :]
            num = _dot(sc.astype(BF16), vb) + inter * _dot(qb, c_h.astype(BF16))
            den = jnp.sum(sc, axis=1, keepdims=True) + inter * jnp.sum(qh * n_h, axis=1, keepdims=True)
            heads.append(_ln(num / jnp.maximum(jnp.abs(den), jnp.exp(-m_t))))
            b_last = bc[T - 1:T, :]
            lw_c = b_last - bc + ig_c
            lw_r = b_last - br + ig_r
            m_new = jnp.maximum(b_last + m_prev, jnp.max(lw_r, axis=1, keepdims=True))
            decay = jnp.exp(b_last + m_prev - m_new)
            wk = kh * jnp.exp(lw_c - m_new)
            c_s[s, h] = decay * c_h + _dot(wk.T.astype(BF16), vb)
            n_s[s, h:h + 1, :] = decay * n_h + jnp.sum(wk, axis=0, keepdims=True)
            m_next = jnp.where(lane == h, m_new, m_next)
        m_s[s] = m_next
        seq_heads.append(jnp.concatenate(heads, axis=1))

    hm = jnp.concatenate(seq_heads, axis=0) * mhg_ref[...]
    hm = hm * _sigmoid(_dot(h1b, wo_ref[...]))
    merged = (_sigmoid(_dot(h1b, wga_ref[...])) * _dot(a.astype(BF16), wpa_ref[...])
              + _sigmoid(_dot(h1b, wgb_ref[...])) * _dot(hm.astype(BF16), wpb_ref[...]))
    mix = _dot(merged.astype(BF16), wout_ref[...])
    x1 = _ln(DEEPNORM_ALPHA * x + g1 * mix) * ln1g_ref[...] + ln1b_ref[...]
    h2 = _ln(x1) * (1.0 + sc2) + sh2
    x1_ref[...] = x1.reshape(nb, T, D_MODEL)
    h2b = h2.astype(BF16)
    h2_ref[...] = h2b.reshape(nb, T, D_MODEL)

    logits = _dot(h2b, wr_ref[...]) + br_ref[...]
    lt = logits.T[0:N_EXPERTS, :]
    eio = lax.broadcasted_iota(jnp.int32, (N_EXPERTS, R), 0)
    vals, idxs = [], []
    for _ in range(TOP_K):
        mx = jnp.max(lt, axis=0, keepdims=True)
        idx = jnp.min(jnp.where(lt == mx, eio, N_EXPERTS), axis=0, keepdims=True)
        vals.append(mx)
        idxs.append(idx)
        lt = jnp.where(eio == idx, -jnp.inf, lt)
    ex = [jnp.exp(vk - vals[0]) for vk in vals]
    tot = ex[0] + ex[1] + ex[2] + ex[3]
    eidx_ref[0] = jnp.concatenate(idxs, axis=0)
    gate_ref[0] = jnp.concatenate([e / tot for e in ex], axis=0)

    @pl.when(j == nj - 1)
    def _():
        hist_ref[...] = pbuf[:, 1:HIST_ROWS, :]
        c_ref[...] = c_s[...]
        n_ref[...] = n_s[...]
        m_ref[...] = m_s[...]


def _mixer(x, mod, hist0, c0, n0, m0, wts, *, T, pos0):
    B, L, D = x.shape
    nj = L // T
    nb = ROW_TILE // T if nj == 1 else 1
    steps = (B // nb) * nj
    kern = functools.partial(_mixer_kernel, T=T, pos0=pos0)
    seq_map = lambda b, j: (b, j, 0)
    st3 = lambda b, j: (b, 0, 0)
    st4 = lambda b, j: (b, 0, 0, 0)
    step3 = lambda b, j: (b * nj + j, 0, 0)
    in_specs = [
        pl.BlockSpec((nb, T, D), seq_map),
        pl.BlockSpec((nb, 6, D), st3),
        pl.BlockSpec((nb, POOL_HIST, D_POOL), st3),
        pl.BlockSpec((nb, N_HEADS, HEAD_DK, HEAD_DV), st4),
        pl.BlockSpec((nb, N_HEADS, HEAD_DK), st3),
        pl.BlockSpec((nb, 1, LANES), st3),
    ] + [_const_spec(w.shape) for w in wts]
    out_specs = [
        pl.BlockSpec((nb, T, D), seq_map),
        pl.BlockSpec((nb, T, D), seq_map),
        pl.BlockSpec((1, TOP_K, nb * T), step3),
        pl.BlockSpec((1, TOP_K, nb * T), step3),
        pl.BlockSpec((nb, POOL_HIST, D_POOL), st3),
        pl.BlockSpec((nb, N_HEADS, HEAD_DK, HEAD_DV), st4),
        pl.BlockSpec((nb, N_HEADS, HEAD_DK), st3),
        pl.BlockSpec((nb, 1, LANES), st3),
    ]
    out_shape = [
        jax.ShapeDtypeStruct((B, L, D), F32),
        jax.ShapeDtypeStruct((B, L, D), BF16),
        jax.ShapeDtypeStruct((steps, TOP_K, nb * T), jnp.int32),
        jax.ShapeDtypeStruct((steps, TOP_K, nb * T), F32),
        jax.ShapeDtypeStruct((B, POOL_HIST, D_POOL), F32),
        jax.ShapeDtypeStruct((B, N_HEADS, HEAD_DK, HEAD_DV), F32),
        jax.ShapeDtypeStruct((B, N_HEADS, HEAD_DK), F32),
        jax.ShapeDtypeStruct((B, 1, LANES), F32),
    ]
    scratch = [pltpu.VMEM((nb, T + HIST_ROWS, D_POOL), F32),
               pltpu.VMEM((nb, N_HEADS, HEAD_DK, HEAD_DV), F32),
               pltpu.VMEM((nb, N_HEADS, HEAD_DK), F32),
               pltpu.VMEM((nb, 1, LANES), F32)]
    return pl.pallas_call(
        kern,
        grid=(B // nb, nj),
        in_specs=in_specs,
        out_specs=out_specs,
        out_shape=out_shape,
        scratch_shapes=scratch,
        compiler_params=pltpu.CompilerParams(
            dimension_semantics=("arbitrary", "arbitrary"), vmem_limit_bytes=VMEM_LIMIT),
        name="mixer",
    )(x, mod, hist0, c0, n0, m0, *wts)


def _pad8(x):
    return jnp.floor((x + (SUBLANES - 1)) * (1.0 / SUBLANES)) * SUBLANES


def _expert_counts_row(oh, T):
    ones = jnp.ones((SUBLANES, T), BF16)
    return lax.dot_general(ones, oh.astype(BF16), (((1,), (1,)), ((), ())), preferred_element_type=F32)[0:1]


def _local_positions(e, T):
    eio = lax.broadcasted_iota(jnp.int32, (N_EXPERTS, T), 0)
    hits = [eio == e[k:k + 1, :] for k in range(TOP_K)]
    oh = sum(h.astype(F32) for h in hits)
    row = lax.broadcasted_iota(jnp.int32, (T, T), 0)
    col = lax.broadcasted_iota(jnp.int32, (T, T), 1)
    before = _dot(oh.astype(BF16), (row < col).astype(BF16))
    len8 = _pad8(_expert_counts_row(oh, T))
    er = lax.broadcasted_iota(jnp.int32, (N_EXPERTS, N_EXPERTS), 0)
    ec = lax.broadcasted_iota(jnp.int32, (N_EXPERTS, N_EXPERTS), 1)
    start = jnp.sum(jnp.where(ec < er, len8, 0.0), axis=1, keepdims=True)
    base = before + start
    return [jnp.sum(jnp.where(h, base, 0.0), axis=0, keepdims=True) for h in hits]


def _count_kernel(e_ref, cnt_ref, *, T, tiles):
    eio = lax.broadcasted_iota(jnp.int32, (LANES, T), 0)
    for b in range(tiles):
        e = e_ref[:, b * T:(b + 1) * T]
        oh = sum((eio == e[k:k + 1, :]).astype(F32) for k in range(TOP_K))
        cnt_ref[b] = _expert_counts_row(oh, T).astype(jnp.int32)


def _tile_counts(eidx, *, T):
    K, N = eidx.shape
    n_tiles = N // T
    tiles = next(t for t in (8, 4, 2, 1) if n_tiles % t == 0)
    return pl.pallas_call(
        functools.partial(_count_kernel, T=T, tiles=tiles),
        grid=(n_tiles // tiles,),
        in_specs=[pl.BlockSpec((K, tiles * T), lambda i: (0, i))],
        out_specs=pl.BlockSpec((tiles, 1, LANES), lambda i: (i, 0, 0)),
        out_shape=jax.ShapeDtypeStruct((n_tiles, 1, LANES), jnp.int32),
        name="tile_counts",
    )(eidx)


def _start_piece_copies(rows_ref, make_copy):
    for p in range(SORT_ROWS // SUBLANES):
        make_copy(p * SUBLANES, pl.multiple_of(rows_ref[0, 0, p], SUBLANES)).start()


def _dispatch_kernel(fill_ref, rows_ref, e_ref, ha_ref, hb_ref, xs_ref, sbuf, zbuf, sem, zsem, *, T, tiles_a):
    i = pl.program_id(0)
    n_tiles = pl.num_programs(0) - 1
    slot = i % 2

    def zero_fill(finish):
        def go(off, size):
            cp = pltpu.make_async_copy(zbuf.at[pl.ds(0, size)], xs_ref.at[pl.ds(off, size)], zsem)
            cp.wait() if finish else cp.start()

        for e in range(N_EXPERTS):
            off, pad = pl.multiple_of(fill_ref[e], SUBLANES), fill_ref[N_EXPERTS + e]
            for bit in range(SUBLANES.bit_length() - 1, MOE_ROWS.bit_length() - 1):
                size = 1 << bit
                pl.when((pad & size) != 0)(functools.partial(go, off, size))
                off = pl.multiple_of(off + (pad & size), SUBLANES)

        def tail(blk, c):
            go(pl.multiple_of(blk * MOE_ROWS, MOE_ROWS), MOE_ROWS)
            return c
        lax.fori_loop(fill_ref[2 * N_EXPERTS], xs_ref.shape[0] // MOE_ROWS, tail, 0)

    @pl.when(i == 0)
    def _():
        zbuf[...] = jnp.zeros_like(zbuf)
        zero_fill(False)
        zero_fill(True)
        sbuf[1] = jnp.zeros(sbuf.shape[1:], F32)

    def all_pieces(s):
        return pltpu.make_async_copy(sbuf.at[s], xs_ref.at[pl.ds(0, SORT_ROWS)], sem.at[s])

    @pl.when(i >= 1)
    def _():
        all_pieces(slot).wait()

    _start_piece_copies(rows_ref, lambda r, g: pltpu.make_async_copy(
        sbuf.at[1 - slot, pl.ds(r, SUBLANES)], xs_ref.at[pl.ds(g, SUBLANES)], sem.at[1 - slot]))

    pos = [p.astype(jnp.int32) for p in _local_positions(e_ref[...], T)]
    rio = lax.broadcasted_iota(jnp.int32, (SORT_ROWS, T), 0)
    pick = (rio == pos[0]) | (rio == pos[1]) | (rio == pos[2]) | (rio == pos[3])
    h = jnp.where(i < tiles_a, ha_ref[...], hb_ref[...])
    sbuf[slot] = _dot(pick.astype(BF16), h)

    @pl.when(i == n_tiles)
    def _():
        all_pieces(1 - slot).wait()


def _dispatch(fill, piece_rows, eidx, h2_a, h2_b, n_rows, *, T):
    D = h2_a.shape[1]
    tiles_a, tiles_b = h2_a.shape[0] // T, h2_b.shape[0] // T
    n_tiles = tiles_a + tiles_b
    grid_spec = pltpu.PrefetchScalarGridSpec(
        num_scalar_prefetch=1,
        grid=(n_tiles + 1,),
        in_specs=[pl.BlockSpec((1, 1, PIECE_LANES), lambda i, fill: (i, 0, 0), memory_space=pltpu.SMEM),
                  pl.BlockSpec((TOP_K, T), lambda i, fill: (0, jnp.minimum(i, n_tiles - 1))),
                  pl.BlockSpec((T, D), lambda i, fill: (jnp.minimum(i, tiles_a - 1), 0)),
                  pl.BlockSpec((T, D), lambda i, fill: (jnp.clip(i - tiles_a, 0, tiles_b - 1), 0))],
        out_specs=pl.BlockSpec(memory_space=pl.ANY),
        scratch_shapes=[pltpu.VMEM((2, SORT_ROWS, D), F32), pltpu.VMEM((MOE_ROWS, D), F32),
                        pltpu.SemaphoreType.DMA((2,)), pltpu.SemaphoreType.DMA(())],
    )
    return pl.pallas_call(
        functools.partial(_dispatch_kernel, T=T, tiles_a=tiles_a),
        grid_spec=grid_spec,
        out_shape=jax.ShapeDtypeStruct((n_rows, D), F32),
        compiler_params=pltpu.CompilerParams(
            dimension_semantics=("arbitrary",), vmem_limit_bytes=VMEM_LIMIT),
        name="dispatch",
    )(fill, piece_rows, eidx, h2_a, h2_b)


def _moe_kernel(be_ref, nu_ref, xs_ref, wgu_ref, bgu_ref, wd_ref, bd_ref, y_ref, wgu_s, wd_s):
    i = pl.program_id(0)
    used = i < nu_ref[0]
    new_expert = jnp.logical_or(i == 0, be_ref[i] != be_ref[jnp.maximum(i - 1, 0)])

    @pl.when(jnp.logical_and(used, new_expert))
    def _():
        def cast_rows(r, c):
            rows = pl.ds(pl.multiple_of(r * LANES, LANES), LANES)
            wgu_s[rows, :] = wgu_ref[0, rows, :].astype(BF16)
            wd_s[rows, :] = wd_ref[0, rows, :].astype(BF16)
            return c
        lax.fori_loop(0, D_FF // LANES, cast_rows, 0)

    @pl.when(used)
    def _():
        x = xs_ref[...].astype(BF16)
        bgu = bgu_ref[0]
        acc = None
        for c in range(D_FF // FF_CHUNK):
            cs = slice(c * FF_CHUNK, (c + 1) * FF_CHUNK)
            ls = slice(D_FF + c * FF_CHUNK, D_FF + (c + 1) * FF_CHUNK)
            glu = jnp.minimum(_dot(x, wgu_s[:, cs]) + bgu[:, cs], SWIGLU_LIMIT)
            lin = jnp.clip(_dot(x, wgu_s[:, ls]) + bgu[:, ls], -SWIGLU_LIMIT, SWIGLU_LIMIT)
            act = glu * _sigmoid(SWIGLU_ALPHA * glu) * (lin + 1.0)
            part = _dot(act.astype(BF16), wd_s[cs, :])
            acc = part if acc is None else acc + part
        y_ref[...] = acc + bd_ref[0]

    @pl.when(jnp.logical_not(used))
    def _():
        def zero_rows(r, c):
            rows = pl.ds(pl.multiple_of(r * LANES, LANES), LANES)
            y_ref[rows, :] = jnp.zeros((LANES, y_ref.shape[1]), F32)
            return c
        lax.fori_loop(0, y_ref.shape[0] // LANES, zero_rows, 0)


def _moe(block_e, n_used, xs, w_gu, b_gu, w_down, b_down):
    P, D = xs.shape
    R = MOE_ROWS
    nb = P // R
    xs_map = lambda i, be, nu: (jnp.minimum(i, nu[0] - 1), 0)
    w_map = lambda i, be, nu: (be[i], 0, 0)
    grid_spec = pltpu.PrefetchScalarGridSpec(
        num_scalar_prefetch=2,
        grid=(nb,),
        in_specs=[
            pl.BlockSpec((R, D), xs_map),
            pl.BlockSpec((1, D, 2 * D_FF), w_map),
            pl.BlockSpec((1, 1, 2 * D_FF), w_map),
            pl.BlockSpec((1, D_FF, D), w_map),
            pl.BlockSpec((1, 1, D), w_map),
        ],
        out_specs=pl.BlockSpec((R, D), lambda i, be, nu: (i, 0)),
        scratch_shapes=[pltpu.VMEM((D, 2 * D_FF), BF16), pltpu.VMEM((D_FF, D), BF16)],
    )
    return pl.pallas_call(
        _moe_kernel,
        grid_spec=grid_spec,
        out_shape=jax.ShapeDtypeStruct((P, D), F32),
        compiler_params=pltpu.CompilerParams(
            dimension_semantics=("arbitrary",), vmem_limit_bytes=VMEM_LIMIT),
        name="moe_experts",
    )(block_e, n_used, xs, w_gu, b_gu, w_down, b_down)


def _combine_kernel(rows_ref, rowsn_ref, e_ref, gate_ref, x1_ref, mod_ref, g_ref, b_ref, y_hbm,
                    x2_ref, ybuf, sem, *, T):
    nj = pl.num_programs(1)
    step = pl.program_id(0) * nj + pl.program_id(1)
    n_steps = pl.num_programs(0) * nj
    slot = step % 2
    S = SORT_ROWS

    def piece_copy(s):
        return lambda r, g: pltpu.make_async_copy(
            y_hbm.at[pl.ds(g, SUBLANES)], ybuf.at[s, pl.ds(r, SUBLANES)], sem.at[s])

    def all_pieces(s):
        return pltpu.make_async_copy(y_hbm.at[pl.ds(0, S)], ybuf.at[s], sem.at[s])

    @pl.when(step == 0)
    def _():
        _start_piece_copies(rows_ref, piece_copy(0))

    all_pieces(slot).wait()
    _start_piece_copies(rowsn_ref, piece_copy(1 - slot))

    pos = _local_positions(e_ref[...], T)
    stacked = jnp.concatenate(pos + [gate_ref[...], jnp.zeros((LANES - 2 * TOP_K, T), F32)], axis=0).T
    pos_c = [stacked[:, k:k + 1].astype(jnp.int32) for k in range(TOP_K)]
    gate_c = [stacked[:, TOP_K + k:TOP_K + k + 1] for k in range(TOP_K)]
    cio = lax.broadcasted_iota(jnp.int32, (T, S), 1)
    w = jnp.where(cio == pos_c[0], gate_c[0], 0.0)
    for k in range(1, TOP_K):
        w = w + jnp.where(cio == pos_c[k], gate_c[k], 0.0)
    w_hi = w.astype(BF16)
    w_lo = (w - w_hi.astype(F32)).astype(BF16)
    y_b = ybuf[slot].astype(BF16)
    ffn = _dot(w_hi, y_b) + _dot(w_lo, y_b)
    bt, lt, d = x1_ref.shape
    g2 = mod_ref[...][:, 5:6, :]
    z = DEEPNORM_ALPHA * x1_ref[...] + g2 * ffn.reshape(bt, lt, d)
    x2_ref[...] = _ln(z) * g_ref[...] + b_ref[...]

    @pl.when(step == n_steps - 1)
    def _():
        all_pieces(1 - slot).wait()


def _combine(piece_rows, tile0, eidx, gates, x1, mod, ln_g, ln_b, y, *, T):
    B, L, D = x1.shape
    bt, lt = (1, T) if L >= T else (T // L, L)
    nj = L // lt
    n_tiles = (B // bt) * nj
    cur = lambda b, j: (0, b * nj + j)
    tab_cur = lambda b, j: (tile0 + b * nj + j, 0, 0)
    tab_nxt = lambda b, j: (tile0 + jnp.minimum(b * nj + j + 1, n_tiles - 1), 0, 0)
    return pl.pallas_call(
        functools.partial(_combine_kernel, T=T),
        grid=(B // bt, nj),
        in_specs=[
            pl.BlockSpec((1, 1, PIECE_LANES), tab_cur, memory_space=pltpu.SMEM),
            pl.BlockSpec((1, 1, PIECE_LANES), tab_nxt, memory_space=pltpu.SMEM),
            pl.BlockSpec((TOP_K, T), cur),
            pl.BlockSpec((TOP_K, T), cur),
            pl.BlockSpec((bt, lt, D), lambda b, j: (b, j, 0)),
            pl.BlockSpec((bt, 6, D), lambda b, j: (b, 0, 0)),
            pl.BlockSpec((1, D), lambda b, j: (0, 0)),
            pl.BlockSpec((1, D), lambda b, j: (0, 0)),
            pl.BlockSpec(memory_space=pl.ANY),
        ],
        out_specs=pl.BlockSpec((bt, lt, D), lambda b, j: (b, j, 0)),
        out_shape=jax.ShapeDtypeStruct((B, L, D), F32),
        scratch_shapes=[pltpu.VMEM((2, SORT_ROWS, D), F32), pltpu.SemaphoreType.DMA((2,))],
        compiler_params=pltpu.CompilerParams(
            dimension_semantics=("arbitrary", "arbitrary"), vmem_limit_bytes=VMEM_LIMIT),
        name="combine",
    )(piece_rows, piece_rows, eidx, gates, x1, mod, ln_g, ln_b, y)


def _routing_tables(cnt, n_blocks):
    R = MOE_ROWS
    n_tiles = cnt.shape[0]
    len8 = (cnt + SUBLANES - 1) // SUBLANES * SUBLANES
    total = jnp.sum(len8, axis=0)
    padded = (total + R - 1) // R * R
    pad_end = jnp.cumsum(padded)
    pad_start = pad_end - padded
    first_row = pad_start[None, :] + jnp.cumsum(len8, axis=0) - len8
    tile_end = jnp.cumsum(len8, axis=1)
    piece0 = jnp.arange(SORT_ROWS // SUBLANES, dtype=jnp.int32) * SUBLANES
    in_run = ((piece0[None, :, None] >= (tile_end - len8)[:, None, :])
              & (piece0[None, :, None] < tile_end[:, None, :]))
    shift = jnp.sum(jnp.where(in_run, (first_row - (tile_end - len8))[:, None, :], 0), axis=-1)
    spare = n_blocks * R + piece0
    rows = jnp.where(jnp.any(in_run, axis=-1), piece0[None, :] + shift, spare[None, :])
    rows = jnp.concatenate([spare[None, :], rows], axis=0)
    rows = jnp.pad(rows, ((0, 0), (0, PIECE_LANES - rows.shape[1])))
    piece_rows = rows.astype(jnp.int32).reshape(n_tiles + 1, 1, PIECE_LANES)
    blk0 = jnp.arange(n_blocks + -(-SORT_ROWS // R), dtype=jnp.int32) * R
    block_e = jnp.sum((blk0[:, None] >= pad_end[None, :]).astype(jnp.int32), axis=1)
    block_e = jnp.minimum(block_e, N_EXPERTS - 1)
    n_used = (pad_end[-1:] // R).astype(jnp.int32)
    fill = jnp.concatenate([pad_start + total, padded - total, n_used]).astype(jnp.int32)
    return piece_rows, block_e, n_used, fill


def _mix_group(x, mod, hist0, c0, n0, m0, mixer_w, *, T, pos0):
    B, L, _ = x.shape
    N = B * L
    m0p = jnp.pad(m0, ((0, 0), (0, LANES - N_HEADS))).reshape(B, 1, LANES)
    x1, h2, eidx, gates, hist, c_new, n_new, m_new = _mixer(
        x, mod, hist0, c0, n0, m0p, mixer_w, T=T, pos0=pos0)
    eidx = eidx.transpose(1, 0, 2).reshape(TOP_K, N)
    gates = gates.transpose(1, 0, 2).reshape(TOP_K, N)
    states = (hist, c_new, n_new, m_new.reshape(B, LANES)[:, :N_HEADS])
    return x1, h2.reshape(N, -1), eidx, gates, states


def kernel(x_prompt, x_sample, state_pool, state_mlstm_C, state_mlstm_n, state_mlstm_m, c_prompt, c_sample,
           w_ada, b_ada, w_in, b_if, w_group, pool_scale, mh_norm_g, w_proj_a, w_proj_b, w_out,
           ln1_g, ln1_b, w_router, b_router, w_gu, b_gu, w_down, b_down, ln2_g, ln2_b):
    assert w_ada.shape[0] == DEPTH == 1
    l = 0
    bp, bs = x_prompt.shape[0], x_sample.shape[0]
    dt = x_prompt.dtype
    row = lambda a: a.reshape(1, -1)

    wi = w_in[l]
    o0 = 0
    parts = []
    for width in (D_POOL, N_HEADS * HEAD_DK, N_HEADS * HEAD_DK, N_HEADS * HEAD_DV, N_HEADS * HEAD_DV):
        parts.append(wi[:, o0:o0 + width].astype(BF16))
        o0 += width
    w_if = jnp.pad(wi[:, o0:o0 + 2 * N_HEADS], ((0, 0), (0, LANES - 2 * N_HEADS))).astype(BF16)
    o0 += 2 * N_HEADS
    w_ga = wi[:, o0:o0 + D_MODEL].astype(BF16)
    w_gb = wi[:, o0 + D_MODEL:o0 + 2 * D_MODEL].astype(BF16)
    w_u, w_q, w_k, w_v, w_o = parts
    b_if_row = jnp.pad(b_if[l], (0, LANES - 2 * N_HEADS)).reshape(1, LANES)
    w_r = jnp.pad(w_router[l], ((0, 0), (0, LANES - N_EXPERTS))).astype(BF16)
    b_r = jnp.pad(b_router[l], (0, LANES - N_EXPERTS), constant_values=NEG_BIG).reshape(1, LANES)
    mixer_w = (w_u, w_q, w_k, w_v, w_o, w_ga, w_gb, w_if, b_if_row,
               w_group[l].astype(BF16), row(pool_scale[l]), row(mh_norm_g[l]),
               w_proj_a[l].astype(BF16), w_proj_b[l].astype(BF16), w_out[l].astype(BF16),
               row(ln1_g[l]), row(ln1_b[l]), w_r, b_r)
    moe_w = (w_gu[l], b_gu[l].reshape(N_EXPERTS, 1, 2 * D_FF),
             w_down[l], b_down[l].reshape(N_EXPERTS, 1, D_MODEL))

    mod = _ada(jnp.concatenate([c_prompt, c_sample], axis=0), w_ada[l], b_ada[l])
    mod = mod.reshape(bp + bs, 6, D_MODEL)

    zeros = lambda *s: jnp.zeros(s, dt)
    x1_p, h2_p, eidx_p, gates_p, st_p = _mix_group(
        x_prompt, mod[:bp], zeros(bp, POOL_HIST, D_POOL), zeros(bp, N_HEADS, HEAD_DK, HEAD_DV),
        zeros(bp, N_HEADS, HEAD_DK), zeros(bp, N_HEADS), mixer_w, T=min(256, x_prompt.shape[1]), pos0=0)
    x1_s, h2_s, eidx_s, gates_s, st_s = _mix_group(
        x_sample, mod[bp:], state_pool[l], state_mlstm_C[l], state_mlstm_n[l], state_mlstm_m[l],
        mixer_w, T=min(256, x_sample.shape[1]), pos0=PAST_LEN)

    n_p, n_s = h2_p.shape[0], h2_s.shape[0]
    eidx = jnp.concatenate([eidx_p, eidx_s], axis=1)
    T = ROW_TILE
    n_tiles = (n_p + n_s) // T
    cnt = _tile_counts(eidx, T=T)[:, 0, :N_EXPERTS]
    max_rows = (n_p + n_s) * TOP_K + n_tiles * N_EXPERTS * (SUBLANES - 1) + N_EXPERTS * (MOE_ROWS - 1)
    n_blocks = -(-max_rows // MOE_ROWS)
    piece_rows, block_e, n_used, fill = _routing_tables(cnt, n_blocks)
    n_rows = (n_blocks + -(-SORT_ROWS // MOE_ROWS)) * MOE_ROWS
    xs = _dispatch(fill, piece_rows, eidx, h2_p, h2_s, n_rows, T=T)
    y = _moe(block_e, n_used, xs, *moe_w)
    ln2 = (row(ln2_g[l]), row(ln2_b[l]))
    yp = _combine(piece_rows, 1, eidx_p, gates_p, x1_p, mod[:bp], *ln2, y, T=T)
    ys = _combine(piece_rows, 1 + n_p // T, eidx_s, gates_s, x1_s, mod[bp:], *ln2, y, T=T)
    st = lambda a: a[None]
    return (yp, ys) + tuple(st(a) for a in st_p) + tuple(st(a) for a in st_s)
```

```python
import functools

import jax
import jax.numpy as jnp
from jax import lax
from jax.experimental import pallas as pl
from jax.experimental.pallas import tpu as pltpu

D_MODEL = 1024
DEPTH = 1
D_POOL = 512
POOL_WINDOWS = (2, 4, 8, 16)
POOL_GROUP = 128
POOL_HIST = 15
N_HEADS = 4
HEAD_DK = 256
HEAD_DV = 256
N_EXPERTS = 32
TOP_K = 4
D_FF = 1024
SWIGLU_ALPHA = 1.702
SWIGLU_LIMIT = 7.0
LN_EPS = 1e-5
DEEPNORM_ALPHA = (2.0 * DEPTH) ** 0.25
PAST_LEN = 4096

LANES = 128
SUBLANES = 8
HIST_ROWS = POOL_HIST + 1
MOE_ROWS = 512
FF_CHUNK = 512
ROW_TILE = 256
SORT_ROWS = TOP_K * ROW_TILE + N_EXPERTS * SUBLANES
PIECE_LANES = -(-(SORT_ROWS // SUBLANES) // LANES) * LANES
VMEM_LIMIT = 56 * 1024 * 1024
NEG_BIG = -1e30

F32 = jnp.float32
BF16 = jnp.bfloat16


def _dot(a, b):
    return jnp.dot(a, b, preferred_element_type=F32)


def _ln(x):
    mu = jnp.mean(x, axis=-1, keepdims=True)
    xc = x - mu
    var = jnp.mean(xc * xc, axis=-1, keepdims=True)
    return xc * lax.rsqrt(var + LN_EPS)


def _sigmoid(x):
    return 0.5 * jnp.tanh(0.5 * x) + 0.5


def _log_sigmoid(x):
    return jnp.minimum(x, 0.0) - jnp.log(1.0 + jnp.exp(-jnp.abs(x)))


def _split3(x):
    hi = x.astype(BF16)
    r1 = x - hi.astype(F32)
    mid = r1.astype(BF16)
    lo = (r1 - mid.astype(F32)).astype(BF16)
    return hi, mid, lo


def _const_spec(shape):
    nd = len(shape)
    return pl.BlockSpec(shape, lambda *_: (0,) * nd, pipeline_mode=pl.Buffered(1))


def _ada_kernel(c_ref, w_ref, b_ref, o_ref):
    c = c_ref[...]
    s = c * _sigmoid(c)
    o_ref[...] = _dot(s.astype(BF16), w_ref[...].astype(BF16)) + b_ref[...]


def _ada(c, w_ada, b_ada):
    nb, d = c.shape
    n_out = w_ada.shape[1]
    blk = D_MODEL
    return pl.pallas_call(
        _ada_kernel,
        grid=(n_out // blk,),
        in_specs=[
            pl.BlockSpec((nb, d), lambda i: (0, 0)),
            pl.BlockSpec((d, blk), lambda i: (0, i)),
            pl.BlockSpec((1, blk), lambda i: (0, i)),
        ],
        out_specs=pl.BlockSpec((nb, blk), lambda i: (0, i)),
        out_shape=jax.ShapeDtypeStruct((nb, n_out), F32),
        name="ada_mod",
    )(c, w_ada, b_ada.reshape(1, n_out))


def _mixer_kernel(x_ref, xp_ref, mod_ref, modp_ref, hist0_ref, c0_ref, n0_ref, m0_ref,
                  wu_ref, wq_ref, wk_ref, wv_ref, wo_ref, wga_ref, wgb_ref, wif_ref, bif_ref,
                  wgrp_ref, pscale_ref, mhg_ref, wpa_ref, wpb_ref, wout_ref, ln1g_ref, ln1b_ref,
                  wr_ref, br_ref,
                  x1_ref, h2_ref, eidx_ref, gate_ref, hist_ref, c_ref, n_ref, m_ref,
                  pbuf, c_s, n_s, m_s, mix_s, *, T, pos0, nj, n_steps):
    i = pl.program_id(0)
    j = lax.rem(jnp.minimum(i, n_steps - 1), nj)
    nb = x_ref.shape[0]
    R = nb * T

    def mod_rows(ref, r):
        if nb == 1:
            return ref[0][r:r + 1]
        return jnp.concatenate([jnp.broadcast_to(ref[s][r:r + 1], (T, D_MODEL)) for s in range(nb)], axis=0)

    @pl.when(i == 0)
    def _():
        mix_s[...] = jnp.zeros_like(mix_s)

    @pl.when(j == 0)
    def _():
        c_s[...] = c0_ref[...]
        n_s[...] = n0_ref[...]
        m_s[...] = m0_ref[...]
        pbuf[:, 0:1, :] = jnp.zeros((nb, 1, D_POOL), F32)
        pbuf[:, 1:HIST_ROWS, :] = hist0_ref[...]


    x = x_ref[...].reshape(R, D_MODEL)
    sh1, sc1 = mod_rows(mod_ref, 0), mod_rows(mod_ref, 1)
    h1b = (_ln(x) * (1.0 + sc1) + sh1).astype(BF16)

    u = _dot(h1b, wu_ref[...])
    pos = pos0 + j * T + lax.broadcasted_iota(jnp.int32, (T, 1), 0)
    for s in range(nb):
        pbuf[s, HIST_ROWS:HIST_ROWS + T, :] = u[s * T:(s + 1) * T]
    mixed = []
    for g, w in enumerate(POOL_WINDOWS):
        cols = slice(g * POOL_GROUP, (g + 1) * POOL_GROUP)
        cnt = jnp.minimum(w, pos + 1).astype(F32)
        wins = []
        for s in range(nb):
            win = pbuf[s, HIST_ROWS:HIST_ROWS + T, cols]
            for back in range(1, w):
                win = win + pbuf[s, HIST_ROWS - back:HIST_ROWS - back + T, cols]
            wins.append(win / cnt)
        pooled = jnp.concatenate(wins, axis=0) - u[:, cols]
        mixed.append(_dot(pooled.astype(BF16), wgrp_ref[g]))
    a = jnp.concatenate(mixed, axis=1) * pscale_ref[...]
    for s in range(nb):
        pbuf[s, 0:HIST_ROWS, :] = pbuf[s, T:T + HIST_ROWS, :]

    gif = _dot(h1b, wif_ref[...]) + bif_ref[...]
    lf = _log_sigmoid(gif)
    row = lax.broadcasted_iota(jnp.int32, (T, T), 0)
    col = lax.broadcasted_iota(jnp.int32, (T, T), 1)
    causal = col <= row
    tri = causal.astype(BF16)
    tri_t = (row <= col).astype(BF16)
    gif_ts = [gif[s * T:(s + 1) * T].T for s in range(nb)]
    b_cs = [sum(_dot(tri, p) for p in _split3(lf[s * T:(s + 1) * T])) for s in range(nb)]
    b_rs = [sum(_dot(p, tri_t) for p in _split3(_log_sigmoid(g))) for g in gif_ts]
    q = _dot(h1b, wq_ref[...])
    k = _dot(h1b, wk_ref[...]) * (HEAD_DK ** -0.5)
    v = _dot(h1b, wv_ref[...])
    lane = lax.broadcasted_iota(jnp.int32, (1, LANES), 1)
    seq_heads = []
    for s in range(nb):
        rs = slice(s * T, (s + 1) * T)
        gif_s, gif_t, b_c, b_r = gif[rs], gif_ts[s], b_cs[s], b_rs[s]
        m_all = m_s[s]
        m_next = m_all
        heads = []
        for h in range(N_HEADS):
            sl = slice(h * HEAD_DK, (h + 1) * HEAD_DK)
            qh, kh = q[rs, sl], k[rs, sl]
            qb, kb, vb = qh.astype(BF16), kh.astype(BF16), v[rs, sl].astype(BF16)
            bc = b_c[:, N_HEADS + h:N_HEADS + h + 1]
            br = b_r[N_HEADS + h:N_HEADS + h + 1, :]
            ig_c = gif_s[:, h:h + 1]
            ig_r = gif_t[h:h + 1, :]
            m_prev = m_all[:, h:h + 1]
            log_d = jnp.where(causal, bc - br + ig_r, -jnp.inf)
            log_inter = bc + m_prev
            m_t = jnp.maximum(log_inter, jnp.max(log_d, axis=1, keepdims=True))
            d = jnp.exp(log_d - m_t)
            sc = lax.dot_general(qb, kb, (((1,), (1,)), ((), ())), preferred_element_type=F32) * d
            inter = jnp.exp(log_inter - m_t)
            c_h = c_s[s, h]
            n_h = n_s[s, h:h + 1, :]
            num = _dot(sc.astype(BF16), vb) + inter * _dot(qb, c_h.astype(BF16))
            den = jnp.sum(sc, axis=1, keepdims=True) + inter * jnp.sum(qh * n_h, axis=1, keepdims=True)
            heads.append(_ln(num / jnp.maximum(jnp.abs(den), jnp.exp(-m_t))))
            b_last = bc[T - 1:T, :]
            lw_c = b_last - bc + ig_c
            lw_r = b_last - br + ig_r
            m_new = jnp.maximum(b_last + m_prev, jnp.max(lw_r, axis=1, keepdims=True))
            decay = jnp.exp(b_last + m_prev - m_new)
            wk = kh * jnp.exp(lw_c - m_new)
            c_s[s, h] = decay * c_h + _dot(wk.T.astype(BF16), vb)
            n_s[s, h:h + 1, :] = decay * n_h + jnp.sum(wk, axis=0, keepdims=True)
            m_next = jnp.where(lane == h, m_new, m_next)
        m_s[s] = m_next
        seq_heads.append(jnp.concatenate(heads, axis=1))

    xp = xp_ref[...].reshape(R, D_MODEL)
    g1, sh2, sc2 = mod_rows(modp_ref, 2), mod_rows(modp_ref, 3), mod_rows(modp_ref, 4)
    x1 = _ln(DEEPNORM_ALPHA * xp + g1 * mix_s[...]) * ln1g_ref[...] + ln1b_ref[...]
    h2 = _ln(x1) * (1.0 + sc2) + sh2
    x1_ref[...] = x1.reshape(nb, T, D_MODEL)
    h2b = h2.astype(BF16)
    h2_ref[...] = h2b.reshape(nb, T, D_MODEL)
    logits = _dot(h2b, wr_ref[...]) + br_ref[...]
    lt = logits.T[0:N_EXPERTS, :]
    eio = lax.broadcasted_iota(jnp.int32, (N_EXPERTS, R), 0)
    vals, idxs = [], []
    for _ in range(TOP_K):
        mx = jnp.max(lt, axis=0, keepdims=True)
        idx = jnp.min(jnp.where(lt == mx, eio, N_EXPERTS), axis=0, keepdims=True)
        vals.append(mx)
        idxs.append(idx)
        lt = jnp.where(eio == idx, -jnp.inf, lt)
    ex = [jnp.exp(vk - vals[0]) for vk in vals]
    tot = ex[0] + ex[1] + ex[2] + ex[3]
    eidx_ref[0] = jnp.concatenate(idxs, axis=0)
    gate_ref[0] = jnp.concatenate([e / tot for e in ex], axis=0)

    hm = jnp.concatenate(seq_heads, axis=0) * mhg_ref[...]
    hm = hm * _sigmoid(_dot(h1b, wo_ref[...]))
    merged = (_sigmoid(_dot(h1b, wga_ref[...])) * _dot(a.astype(BF16), wpa_ref[...])
              + _sigmoid(_dot(h1b, wgb_ref[...])) * _dot(hm.astype(BF16), wpb_ref[...]))
    mix_s[...] = _dot(merged.astype(BF16), wout_ref[...])

    @pl.when(jnp.logical_and(j == nj - 1, i < n_steps))
    def _():
        hist_ref[...] = pbuf[:, 1:HIST_ROWS, :]
        c_ref[...] = c_s[...]
        n_ref[...] = n_s[...]
        m_ref[...] = m_s[...]


def _mixer(x, mod, hist0, c0, n0, m0, wts, *, T, pos0):
    B, L, D = x.shape
    nj = L // T
    nb = ROW_TILE // T if nj == 1 else 1
    steps = (B // nb) * nj
    kern = functools.partial(_mixer_kernel, T=T, pos0=pos0, nj=nj, n_steps=steps)
    cur = lambda i: jnp.minimum(i, steps - 1)
    prev = lambda i: jnp.maximum(i - 1, 0)
    seq_map = lambda i: (cur(i) // nj, cur(i) % nj, 0)
    seqp_map = lambda i: (prev(i) // nj, prev(i) % nj, 0)
    st3 = lambda i: (cur(i) // nj, 0, 0)
    stp3 = lambda i: (prev(i) // nj, 0, 0)
    st4 = lambda i: (cur(i) // nj, 0, 0, 0)
    step3 = lambda i: (prev(i), 0, 0)
    in_specs = [
        pl.BlockSpec((nb, T, D), seq_map),
        pl.BlockSpec((nb, T, D), seqp_map),
        pl.BlockSpec((nb, 6, D), st3),
        pl.BlockSpec((nb, 6, D), stp3),
        pl.BlockSpec((nb, POOL_HIST, D_POOL), st3),
        pl.BlockSpec((nb, N_HEADS, HEAD_DK, HEAD_DV), st4, pipeline_mode=pl.Buffered(1 if nb > 1 else 2)),
        pl.BlockSpec((nb, N_HEADS, HEAD_DK), st3),
        pl.BlockSpec((nb, 1, LANES), st3),
    ] + [_const_spec(w.shape) for w in wts]
    out_specs = [
        pl.BlockSpec((nb, T, D), seqp_map),
        pl.BlockSpec((nb, T, D), seqp_map),
        pl.BlockSpec((1, TOP_K, nb * T), step3),
        pl.BlockSpec((1, TOP_K, nb * T), step3),
        pl.BlockSpec((nb, POOL_HIST, D_POOL), st3),
        pl.BlockSpec((nb, N_HEADS, HEAD_DK, HEAD_DV), st4),
        pl.BlockSpec((nb, N_HEADS, HEAD_DK), st3),
        pl.BlockSpec((nb, 1, LANES), st3),
    ]
    out_shape = [
        jax.ShapeDtypeStruct((B, L, D), F32),
        jax.ShapeDtypeStruct((B, L, D), BF16),
        jax.ShapeDtypeStruct((steps, TOP_K, nb * T), jnp.int32),
        jax.ShapeDtypeStruct((steps, TOP_K, nb * T), F32),
        jax.ShapeDtypeStruct((B, POOL_HIST, D_POOL), F32),
        jax.ShapeDtypeStruct((B, N_HEADS, HEAD_DK, HEAD_DV), F32),
        jax.ShapeDtypeStruct((B, N_HEADS, HEAD_DK), F32),
        jax.ShapeDtypeStruct((B, 1, LANES), F32),
    ]
    scratch = [pltpu.VMEM((nb, T + HIST_ROWS, D_POOL), F32),
               pltpu.VMEM((nb, N_HEADS, HEAD_DK, HEAD_DV), F32),
               pltpu.VMEM((nb, N_HEADS, HEAD_DK), F32),
               pltpu.VMEM((nb, 1, LANES), F32),
               pltpu.VMEM((nb * T, D), F32)]
    return pl.pallas_call(
        kern,
        grid=(steps + 1,),
        in_specs=in_specs,
        out_specs=out_specs,
        out_shape=out_shape,
        scratch_shapes=scratch,
        compiler_params=pltpu.CompilerParams(
            dimension_semantics=("arbitrary",), vmem_limit_bytes=VMEM_LIMIT),
        name="mixer",
    )(x, x, mod, mod, hist0, c0, n0, m0, *wts)


def _pad8(x):
    return jnp.floor((x + (SUBLANES - 1)) * (1.0 / SUBLANES)) * SUBLANES


def _expert_counts_row(oh, T):
    ones = jnp.ones((SUBLANES, T), BF16)
    return lax.dot_general(ones, oh.astype(BF16), (((1,), (1,)), ((), ())), preferred_element_type=F32)[0:1]


def _local_positions(e, T):
    eio = lax.broadcasted_iota(jnp.int32, (N_EXPERTS, T), 0)
    hits = [eio == e[k:k + 1, :] for k in range(TOP_K)]
    oh = sum(h.astype(F32) for h in hits)
    row = lax.broadcasted_iota(jnp.int32, (T, T), 0)
    col = lax.broadcasted_iota(jnp.int32, (T, T), 1)
    before = _dot(oh.astype(BF16), (row < col).astype(BF16))
    len8 = _pad8(_expert_counts_row(oh, T))
    er = lax.broadcasted_iota(jnp.int32, (N_EXPERTS, N_EXPERTS), 0)
    ec = lax.broadcasted_iota(jnp.int32, (N_EXPERTS, N_EXPERTS), 1)
    start = jnp.sum(jnp.where(ec < er, len8, 0.0), axis=1, keepdims=True)
    base = before + start
    return [jnp.sum(jnp.where(h, base, 0.0), axis=0, keepdims=True) for h in hits]


def _count_kernel(e_ref, cnt_ref, *, T, tiles):
    eio = lax.broadcasted_iota(jnp.int32, (LANES, T), 0)
    for b in range(tiles):
        e = e_ref[:, b * T:(b + 1) * T]
        oh = sum((eio == e[k:k + 1, :]).astype(F32) for k in range(TOP_K))
        cnt_ref[b] = _expert_counts_row(oh, T).astype(jnp.int32)


def _tile_counts(eidx, *, T):
    K, N = eidx.shape
    n_tiles = N // T
    tiles = next(t for t in (8, 4, 2, 1) if n_tiles % t == 0)
    return pl.pallas_call(
        functools.partial(_count_kernel, T=T, tiles=tiles),
        grid=(n_tiles // tiles,),
        in_specs=[pl.BlockSpec((K, tiles * T), lambda i: (0, i))],
        out_specs=pl.BlockSpec((tiles, 1, LANES), lambda i: (i, 0, 0)),
        out_shape=jax.ShapeDtypeStruct((n_tiles, 1, LANES), jnp.int32),
        name="tile_counts",
    )(eidx)


def _start_piece_copies(rows_ref, make_copy):
    for p in range(SORT_ROWS // SUBLANES):
        make_copy(p * SUBLANES, pl.multiple_of(rows_ref[0, 0, p], SUBLANES)).start()


def _dispatch_kernel(fill_ref, rows_ref, e_ref, ha_ref, hb_ref, xs_ref, sbuf, zbuf, sem, zsem, *, T, tiles_a):
    i = pl.program_id(0)
    n_tiles = pl.num_programs(0) - 1
    slot = i % 2

    def zero_fill(finish):
        def go(off, size):
            cp = pltpu.make_async_copy(zbuf.at[pl.ds(0, size)], xs_ref.at[pl.ds(off, size)], zsem)
            cp.wait() if finish else cp.start()

        for e in range(N_EXPERTS):
            off, pad = pl.multiple_of(fill_ref[e], SUBLANES), fill_ref[N_EXPERTS + e]
            for bit in range(SUBLANES.bit_length() - 1, MOE_ROWS.bit_length() - 1):
                size = 1 << bit
                pl.when((pad & size) != 0)(functools.partial(go, off, size))
                off = pl.multiple_of(off + (pad & size), SUBLANES)

        def tail(blk, c):
            go(pl.multiple_of(blk * MOE_ROWS, MOE_ROWS), MOE_ROWS)
            return c
        lax.fori_loop(fill_ref[2 * N_EXPERTS], xs_ref.shape[0] // MOE_ROWS, tail, 0)

    @pl.when(i == 0)
    def _():
        zbuf[...] = jnp.zeros_like(zbuf)
        zero_fill(False)
        zero_fill(True)
        sbuf[1] = jnp.zeros(sbuf.shape[1:], F32)

    def all_pieces(s):
        return pltpu.make_async_copy(sbuf.at[s], xs_ref.at[pl.ds(0, SORT_ROWS)], sem.at[s])

    @pl.when(i >= 1)
    def _():
        all_pieces(slot).wait()

    _start_piece_copies(rows_ref, lambda r, g: pltpu.make_async_copy(
        sbuf.at[1 - slot, pl.ds(r, SUBLANES)], xs_ref.at[pl.ds(g, SUBLANES)], sem.at[1 - slot]))

    pos = [p.astype(jnp.int32) for p in _local_positions(e_ref[...], T)]
    rio = lax.broadcasted_iota(jnp.int32, (SORT_ROWS, T), 0)
    pick = (rio == pos[0]) | (rio == pos[1]) | (rio == pos[2]) | (rio == pos[3])
    h = jnp.where(i < tiles_a, ha_ref[...], hb_ref[...])
    sbuf[slot] = _dot(pick.astype(BF16), h)

    @pl.when(i == n_tiles)
    def _():
        all_pieces(1 - slot).wait()


def _dispatch(fill, piece_rows, eidx, h2_a, h2_b, n_rows, *, T):
    D = h2_a.shape[1]
    tiles_a, tiles_b = h2_a.shape[0] // T, h2_b.shape[0] // T
    n_tiles = tiles_a + tiles_b
    grid_spec = pltpu.PrefetchScalarGridSpec(
        num_scalar_prefetch=1,
        grid=(n_tiles + 1,),
        in_specs=[pl.BlockSpec((1, 1, PIECE_LANES), lambda i, fill: (i, 0, 0), memory_space=pltpu.SMEM),
                  pl.BlockSpec((TOP_K, T), lambda i, fill: (0, jnp.minimum(i, n_tiles - 1))),
                  pl.BlockSpec((T, D), lambda i, fill: (jnp.minimum(i, tiles_a - 1), 0)),
                  pl.BlockSpec((T, D), lambda i, fill: (jnp.clip(i - tiles_a, 0, tiles_b - 1), 0))],
        out_specs=pl.BlockSpec(memory_space=pl.ANY),
        scratch_shapes=[pltpu.VMEM((2, SORT_ROWS, D), F32), pltpu.VMEM((MOE_ROWS, D), F32),
                        pltpu.SemaphoreType.DMA((2,)), pltpu.SemaphoreType.DMA(())],
    )
    return pl.pallas_call(
        functools.partial(_dispatch_kernel, T=T, tiles_a=tiles_a),
        grid_spec=grid_spec,
        out_shape=jax.ShapeDtypeStruct((n_rows, D), F32),
        compiler_params=pltpu.CompilerParams(
            dimension_semantics=("arbitrary",), vmem_limit_bytes=VMEM_LIMIT),
        name="dispatch",
    )(fill, piece_rows, eidx, h2_a, h2_b)


def _moe_kernel(be_ref, nu_ref, xs_ref, wgu_ref, bgu_ref, wd_ref, bd_ref, y_ref, wgu_s, wd_s):
    i = pl.program_id(0)
    used = i < nu_ref[0]
    new_expert = jnp.logical_or(i == 0, be_ref[i] != be_ref[jnp.maximum(i - 1, 0)])

    @pl.when(jnp.logical_and(used, new_expert))
    def _():
        def cast_rows(r, c):
            rows = pl.ds(pl.multiple_of(r * LANES, LANES), LANES)
            wgu_s[rows, :] = wgu_ref[0, rows, :].astype(BF16)
            wd_s[rows, :] = wd_ref[0, rows, :].astype(BF16)
            return c
        lax.fori_loop(0, D_FF // LANES, cast_rows, 0)

    @pl.when(used)
    def _():
        x = xs_ref[...].astype(BF16)
        bgu = bgu_ref[0]
        acc = None
        for c in range(D_FF // FF_CHUNK):
            cs = slice(c * FF_CHUNK, (c + 1) * FF_CHUNK)
            ls = slice(D_FF + c * FF_CHUNK, D_FF + (c + 1) * FF_CHUNK)
            glu = jnp.minimum(_dot(x, wgu_s[:, cs]) + bgu[:, cs], SWIGLU_LIMIT)
            lin = jnp.clip(_dot(x, wgu_s[:, ls]) + bgu[:, ls], -SWIGLU_LIMIT, SWIGLU_LIMIT)
            act = glu * _sigmoid(SWIGLU_ALPHA * glu) * (lin + 1.0)
            part = _dot(act.astype(BF16), wd_s[cs, :])
            acc = part if acc is None else acc + part
        y_ref[...] = acc + bd_ref[0]

    @pl.when(jnp.logical_not(used))
    def _():
        def zero_rows(r, c):
            rows = pl.ds(pl.multiple_of(r * LANES, LANES), LANES)
            y_ref[rows, :] = jnp.zeros((LANES, y_ref.shape[1]), F32)
            return c
        lax.fori_loop(0, y_ref.shape[0] // LANES, zero_rows, 0)


def _moe(block_e, n_used, xs, w_gu, b_gu, w_down, b_down):
    P, D = xs.shape
    R = MOE_ROWS
    nb = P // R
    xs_map = lambda i, be, nu: (jnp.minimum(i, nu[0] - 1), 0)
    w_map = lambda i, be, nu: (be[i], 0, 0)
    grid_spec = pltpu.PrefetchScalarGridSpec(
        num_scalar_prefetch=2,
        grid=(nb,),
        in_specs=[
            pl.BlockSpec((R, D), xs_map),
            pl.BlockSpec((1, D, 2 * D_FF), w_map),
            pl.BlockSpec((1, 1, 2 * D_FF), w_map),
            pl.BlockSpec((1, D_FF, D), w_map),
            pl.BlockSpec((1, 1, D), w_map),
        ],
        out_specs=pl.BlockSpec((R, D), lambda i, be, nu: (i, 0)),
        scratch_shapes=[pltpu.VMEM((D, 2 * D_FF), BF16), pltpu.VMEM((D_FF, D), BF16)],
    )
    return pl.pallas_call(
        _moe_kernel,
        grid_spec=grid_spec,
        out_shape=jax.ShapeDtypeStruct((P, D), F32),
        compiler_params=pltpu.CompilerParams(
            dimension_semantics=("arbitrary",), vmem_limit_bytes=VMEM_LIMIT),
        name="moe_experts",
    )(block_e, n_used, xs, w_gu, b_gu, w_down, b_down)


def _combine_kernel(rows_ref, rowsn_ref, e_ref, gate_ref, x1_ref, mod_ref, g_ref, b_ref, y_hbm,
                    x2_ref, ybuf, sem, *, T):
    nj = pl.num_programs(1)
    step = pl.program_id(0) * nj + pl.program_id(1)
    n_steps = pl.num_programs(0) * nj
    slot = step % 2
    S = SORT_ROWS

    def piece_copy(s):
        return lambda r, g: pltpu.make_async_copy(
            y_hbm.at[pl.ds(g, SUBLANES)], ybuf.at[s, pl.ds(r, SUBLANES)], sem.at[s])

    def all_pieces(s):
        return pltpu.make_async_copy(y_hbm.at[pl.ds(0, S)], ybuf.at[s], sem.at[s])

    @pl.when(step == 0)
    def _():
        _start_piece_copies(rows_ref, piece_copy(0))

    all_pieces(slot).wait()
    _start_piece_copies(rowsn_ref, piece_copy(1 - slot))

    pos = _local_positions(e_ref[...], T)
    stacked = jnp.concatenate(pos + [gate_ref[...], jnp.zeros((LANES - 2 * TOP_K, T), F32)], axis=0).T
    pos_c = [stacked[:, k:k + 1].astype(jnp.int32) for k in range(TOP_K)]
    gate_c = [stacked[:, TOP_K + k:TOP_K + k + 1] for k in range(TOP_K)]
    cio = lax.broadcasted_iota(jnp.int32, (T, S), 1)
    w = jnp.where(cio == pos_c[0], gate_c[0], 0.0)
    for k in range(1, TOP_K):
        w = w + jnp.where(cio == pos_c[k], gate_c[k], 0.0)
    w_hi = w.astype(BF16)
    w_lo = (w - w_hi.astype(F32)).astype(BF16)
    y_b = ybuf[slot].astype(BF16)
    ffn = _dot(w_hi, y_b) + _dot(w_lo, y_b)
    bt, lt, d = x1_ref.shape
    g2 = mod_ref[...][:, 5:6, :]
    z = DEEPNORM_ALPHA * x1_ref[...] + g2 * ffn.reshape(bt, lt, d)
    x2_ref[...] = _ln(z) * g_ref[...] + b_ref[...]

    @pl.when(step == n_steps - 1)
    def _():
        all_pieces(1 - slot).wait()


def _combine(piece_rows, tile0, eidx, gates, x1, mod, ln_g, ln_b, y, *, T):
    B, L, D = x1.shape
    bt, lt = (1, T) if L >= T else (T // L, L)
    nj = L // lt
    n_tiles = (B // bt) * nj
    cur = lambda b, j: (0, b * nj + j)
    tab_cur = lambda b, j: (tile0 + b * nj + j, 0, 0)
    tab_nxt = lambda b, j: (tile0 + jnp.minimum(b * nj + j + 1, n_tiles - 1), 0, 0)
    return pl.pallas_call(
        functools.partial(_combine_kernel, T=T),
        grid=(B // bt, nj),
        in_specs=[
            pl.BlockSpec((1, 1, PIECE_LANES), tab_cur, memory_space=pltpu.SMEM),
            pl.BlockSpec((1, 1, PIECE_LANES), tab_nxt, memory_space=pltpu.SMEM),
            pl.BlockSpec((TOP_K, T), cur),
            pl.BlockSpec((TOP_K, T), cur),
            pl.BlockSpec((bt, lt, D), lambda b, j: (b, j, 0)),
            pl.BlockSpec((bt, 6, D), lambda b, j: (b, 0, 0)),
            pl.BlockSpec((1, D), lambda b, j: (0, 0)),
            pl.BlockSpec((1, D), lambda b, j: (0, 0)),
            pl.BlockSpec(memory_space=pl.ANY),
        ],
        out_specs=pl.BlockSpec((bt, lt, D), lambda b, j: (b, j, 0)),
        out_shape=jax.ShapeDtypeStruct((B, L, D), F32),
        scratch_shapes=[pltpu.VMEM((2, SORT_ROWS, D), F32), pltpu.SemaphoreType.DMA((2,))],
        compiler_params=pltpu.CompilerParams(
            dimension_semantics=("arbitrary", "arbitrary"), vmem_limit_bytes=VMEM_LIMIT),
        name="combine",
    )(piece_rows, piece_rows, eidx, gates, x1, mod, ln_g, ln_b, y)


def _routing_tables(cnt, n_blocks):
    R = MOE_ROWS
    n_tiles = cnt.shape[0]
    len8 = (cnt + SUBLANES - 1) // SUBLANES * SUBLANES
    total = jnp.sum(len8, axis=0)
    padded = (total + R - 1) // R * R
    pad_end = jnp.cumsum(padded)
    pad_start = pad_end - padded
    first_row = pad_start[None, :] + jnp.cumsum(len8, axis=0) - len8
    tile_end = jnp.cumsum(len8, axis=1)
    piece0 = jnp.arange(SORT_ROWS // SUBLANES, dtype=jnp.int32) * SUBLANES
    in_run = ((piece0[None, :, None] >= (tile_end - len8)[:, None, :])
              & (piece0[None, :, None] < tile_end[:, None, :]))
    shift = jnp.sum(jnp.where(in_run, (first_row - (tile_end - len8))[:, None, :], 0), axis=-1)
    spare = n_blocks * R + piece0
    rows = jnp.where(jnp.any(in_run, axis=-1), piece0[None, :] + shift, spare[None, :])
    rows = jnp.concatenate([spare[None, :], rows], axis=0)
    rows = jnp.pad(rows, ((0, 0), (0, PIECE_LANES - rows.shape[1])))
    piece_rows = rows.astype(jnp.int32).reshape(n_tiles + 1, 1, PIECE_LANES)
    blk0 = jnp.arange(n_blocks + -(-SORT_ROWS // R), dtype=jnp.int32) * R
    block_e = jnp.sum((blk0[:, None] >= pad_end[None, :]).astype(jnp.int32), axis=1)
    block_e = jnp.minimum(block_e, N_EXPERTS - 1)
    n_used = (pad_end[-1:] // R).astype(jnp.int32)
    fill = jnp.concatenate([pad_start + total, padded - total, n_used]).astype(jnp.int32)
    return piece_rows, block_e, n_used, fill


def _mix_group(x, mod, hist0, c0, n0, m0, mixer_w, *, T, pos0):
    B, L, _ = x.shape
    N = B * L
    m0p = jnp.pad(m0, ((0, 0), (0, LANES - N_HEADS))).reshape(B, 1, LANES)
    x1, h2, eidx, gates, hist, c_new, n_new, m_new = _mixer(
        x, mod, hist0, c0, n0, m0p, mixer_w, T=T, pos0=pos0)
    eidx = eidx.transpose(1, 0, 2).reshape(TOP_K, N)
    gates = gates.transpose(1, 0, 2).reshape(TOP_K, N)
    states = (hist, c_new, n_new, m_new.reshape(B, LANES)[:, :N_HEADS])
    return x1, h2.reshape(N, -1), eidx, gates, states


def kernel(x_prompt, x_sample, state_pool, state_mlstm_C, state_mlstm_n, state_mlstm_m, c_prompt, c_sample,
           w_ada, b_ada, w_in, b_if, w_group, pool_scale, mh_norm_g, w_proj_a, w_proj_b, w_out,
           ln1_g, ln1_b, w_router, b_router, w_gu, b_gu, w_down, b_down, ln2_g, ln2_b):
    assert w_ada.shape[0] == DEPTH == 1
    l = 0
    bp, bs = x_prompt.shape[0], x_sample.shape[0]
    dt = x_prompt.dtype
    row = lambda a: a.reshape(1, -1)

    wi = w_in[l]
    o0 = 0
    parts = []
    for width in (D_POOL, N_HEADS * HEAD_DK, N_HEADS * HEAD_DK, N_HEADS * HEAD_DV, N_HEADS * HEAD_DV):
        parts.append(wi[:, o0:o0 + width].astype(BF16))
        o0 += width
    w_if = jnp.pad(wi[:, o0:o0 + 2 * N_HEADS], ((0, 0), (0, LANES - 2 * N_HEADS))).astype(BF16)
    o0 += 2 * N_HEADS
    w_ga = wi[:, o0:o0 + D_MODEL].astype(BF16)
    w_gb = wi[:, o0 + D_MODEL:o0 + 2 * D_MODEL].astype(BF16)
    w_u, w_q, w_k, w_v, w_o = parts
    b_if_row = jnp.pad(b_if[l], (0, LANES - 2 * N_HEADS)).reshape(1, LANES)
    w_r = jnp.pad(w_router[l], ((0, 0), (0, LANES - N_EXPERTS))).astype(BF16)
    b_r = jnp.pad(b_router[l], (0, LANES - N_EXPERTS), constant_values=NEG_BIG).reshape(1, LANES)
    mixer_w = (w_u, w_q, w_k, w_v, w_o, w_ga, w_gb, w_if, b_if_row,
               w_group[l].astype(BF16), row(pool_scale[l]), row(mh_norm_g[l]),
               w_proj_a[l].astype(BF16), w_proj_b[l].astype(BF16), w_out[l].astype(BF16),
               row(ln1_g[l]), row(ln1_b[l]), w_r, b_r)
    moe_w = (w_gu[l], b_gu[l].reshape(N_EXPERTS, 1, 2 * D_FF),
             w_down[l], b_down[l].reshape(N_EXPERTS, 1, D_MODEL))

    mod = _ada(jnp.concatenate([c_prompt, c_sample], axis=0), w_ada[l], b_ada[l])
    mod = mod.reshape(bp + bs, 6, D_MODEL)

    zeros = lambda *s: jnp.zeros(s, dt)
    x1_p, h2_p, eidx_p, gates_p, st_p = _mix_group(
        x_prompt, mod[:bp], zeros(bp, POOL_HIST, D_POOL), zeros(bp, N_HEADS, HEAD_DK, HEAD_DV),
        zeros(bp, N_HEADS, HEAD_DK), zeros(bp, N_HEADS), mixer_w, T=min(256, x_prompt.shape[1]), pos0=0)
    x1_s, h2_s, eidx_s, gates_s, st_s = _mix_group(
        x_sample, mod[bp:], state_pool[l], state_mlstm_C[l], state_mlstm_n[l], state_mlstm_m[l],
        mixer_w, T=min(256, x_sample.shape[1]), pos0=PAST_LEN)

    n_p, n_s = h2_p.shape[0], h2_s.shape[0]
    eidx = jnp.concatenate([eidx_p, eidx_s], axis=1)
    T = ROW_TILE
    n_tiles = (n_p + n_s) // T
    cnt = _tile_counts(eidx, T=T)[:, 0, :N_EXPERTS]
    max_rows = (n_p + n_s) * TOP_K + n_tiles * N_EXPERTS * (SUBLANES - 1) + N_EXPERTS * (MOE_ROWS - 1)
    n_blocks = -(-max_rows // MOE_ROWS)
    piece_rows, block_e, n_used, fill = _routing_tables(cnt, n_blocks)
    n_rows = (n_blocks + -(-SORT_ROWS // MOE_ROWS)) * MOE_ROWS
    xs = _dispatch(fill, piece_rows, eidx, h2_p, h2_s, n_rows, T=T)
    y = _moe(block_e, n_used, xs, *moe_w)
    ln2 = (row(ln2_g[l]), row(ln2_b[l]))
    yp = _combine(piece_rows, 1, eidx_p, gates_p, x1_p, mod[:bp], *ln2, y, T=T)
    ys = _combine(piece_rows, 1 + n_p // T, eidx_s, gates_s, x1_s, mod[bp:], *ln2, y, T=T)
    st = lambda a: a[None]
    return (yp, ys) + tuple(st(a) for a in st_p) + tuple(st(a) for a in st_s)
```

```python
import functools

import jax
import jax.numpy as jnp
from jax import lax
from jax.experimental import pallas as pl
from jax.experimental.pallas import tpu as pltpu

D_MODEL = 1024
DEPTH = 1
D_POOL = 512
POOL_WINDOWS = (2, 4, 8, 16)
POOL_GROUP = 128
POOL_HIST = 15
N_HEADS = 4
HEAD_DK = 256
HEAD_DV = 256
N_EXPERTS = 32
TOP_K = 4
D_FF = 1024
SWIGLU_ALPHA = 1.702
SWIGLU_LIMIT = 7.0
LN_EPS = 1e-5
DEEPNORM_ALPHA = (2.0 * DEPTH) ** 0.25
PAST_LEN = 4096

LANES = 128
SUBLANES = 8
HIST_ROWS = POOL_HIST + 1
MOE_ROWS = 512
FF_CHUNK = 512
ROW_TILE = 256
SORT_ROWS = TOP_K * ROW_TILE + N_EXPERTS * SUBLANES
PIECE_LANES = -(-(SORT_ROWS // SUBLANES) // LANES) * LANES
VMEM_LIMIT = 56 * 1024 * 1024
NEG_BIG = -1e30

F32 = jnp.float32
BF16 = jnp.bfloat16


def _dot(a, b):
    return jnp.dot(a, b, preferred_element_type=F32)


def _ln(x):
    mu = jnp.mean(x, axis=-1, keepdims=True)
    xc = x - mu
    var = jnp.mean(xc * xc, axis=-1, keepdims=True)
    return xc * lax.rsqrt(var + LN_EPS)


def _sigmoid(x):
    return 0.5 * jnp.tanh(0.5 * x) + 0.5


def _log_sigmoid(x):
    return jnp.minimum(x, 0.0) - jnp.log(1.0 + jnp.exp(-jnp.abs(x)))


def _split3(x):
    hi = x.astype(BF16)
    r1 = x - hi.astype(F32)
    mid = r1.astype(BF16)
    lo = (r1 - mid.astype(F32)).astype(BF16)
    return hi, mid, lo


def _const_spec(shape):
    nd = len(shape)
    return pl.BlockSpec(shape, lambda *_: (0,) * nd, pipeline_mode=pl.Buffered(1))


def _ada_kernel(c_ref, w_ref, b_ref, o_ref):
    c = c_ref[...]
    s = c * _sigmoid(c)
    o_ref[...] = _dot(s.astype(BF16), w_ref[...].astype(BF16)) + b_ref[...]


def _ada(c, w_ada, b_ada):
    nb, d = c.shape
    n_out = w_ada.shape[1]
    blk = D_MODEL
    return pl.pallas_call(
        _ada_kernel,
        grid=(n_out // blk,),
        in_specs=[
            pl.BlockSpec((nb, d), lambda i: (0, 0)),
            pl.BlockSpec((d, blk), lambda i: (0, i)),
            pl.BlockSpec((1, blk), lambda i: (0, i)),
        ],
        out_specs=pl.BlockSpec((nb, blk), lambda i: (0, i)),
        out_shape=jax.ShapeDtypeStruct((nb, n_out), F32),
        name="ada_mod",
    )(c, w_ada, b_ada.reshape(1, n_out))


def _mixer_kernel(x_ref, xp_ref, mod_ref, modp_ref, hist0_ref, c0_ref, n0_ref, m0_ref,
                  wu_ref, wq_ref, wk_ref, wv_ref, wo_ref, wga_ref, wgb_ref, wif_ref, bif_ref,
                  wgrp_ref, pscale_ref, mhg_ref, wpa_ref, wpb_ref, wout_ref, ln1g_ref, ln1b_ref,
                  wr_ref, br_ref,
                  x1_ref, h2_ref, eidx_ref, gate_ref, hist_ref, c_ref, n_ref, m_ref,
                  pbuf, c_s, n_s, m_s, mix_s, *, T, pos0, nj, n_steps):
    i = pl.program_id(0)
    j = lax.rem(jnp.minimum(i, n_steps - 1), nj)
    nb = x_ref.shape[0]
    R = nb * T

    def mod_rows(ref, r):
        if nb == 1:
            return ref[0][r:r + 1]
        return jnp.concatenate([jnp.broadcast_to(ref[s][r:r + 1], (T, D_MODEL)) for s in range(nb)], axis=0)

    @pl.when(i == 0)
    def _():
        mix_s[...] = jnp.zeros_like(mix_s)

    @pl.when(j == 0)
    def _():
        c_s[...] = c0_ref[...]
        n_s[...] = n0_ref[...]
        m_s[...] = m0_ref[...]
        pbuf[:, 0:1, :] = jnp.zeros((nb, 1, D_POOL), F32)
        pbuf[:, 1:HIST_ROWS, :] = hist0_ref[...]


    x = x_ref[...].reshape(R, D_MODEL)
    sh1, sc1 = mod_rows(mod_ref, 0), mod_rows(mod_ref, 1)
    h1b = (_ln(x) * (1.0 + sc1) + sh1).astype(BF16)

    u = _dot(h1b, wu_ref[...])
    pos = pos0 + j * T + lax.broadcasted_iota(jnp.int32, (T, 1), 0)
    for s in range(nb):
        pbuf[s, HIST_ROWS:HIST_ROWS + T, :] = u[s * T:(s + 1) * T]
    mixed = []
    for g, w in enumerate(POOL_WINDOWS):
        cols = slice(g * POOL_GROUP, (g + 1) * POOL_GROUP)
        cnt = jnp.minimum(w, pos + 1).astype(F32)
        wins = []
        for s in range(nb):
            win = pbuf[s, HIST_ROWS:HIST_ROWS + T, cols]
            for back in range(1, w):
                win = win + pbuf[s, HIST_ROWS - back:HIST_ROWS - back + T, cols]
            wins.append(win / cnt)
        pooled = jnp.concatenate(wins, axis=0) - u[:, cols]
        mixed.append(_dot(pooled.astype(BF16), wgrp_ref[g]))
    a = jnp.concatenate(mixed, axis=1) * pscale_ref[...]
    for s in range(nb):
        pbuf[s, 0:HIST_ROWS, :] = pbuf[s, T:T + HIST_ROWS, :]

    gif = _dot(h1b, wif_ref[...]) + bif_ref[...]
    lf = _log_sigmoid(gif)
    row = lax.broadcasted_iota(jnp.int32, (T, T), 0)
    col = lax.broadcasted_iota(jnp.int32, (T, T), 1)
    causal = col <= row
    tri = causal.astype(BF16)
    tri_t = (row <= col).astype(BF16)
    gif_ts = [gif[s * T:(s + 1) * T].T for s in range(nb)]
    b_cs = [sum(_dot(tri, p) for p in _split3(lf[s * T:(s + 1) * T])) for s in range(nb)]
    b_rs = [sum(_dot(p, tri_t) for p in _split3(_log_sigmoid(g))) for g in gif_ts]
    q = _dot(h1b, wq_ref[...])
    k = _dot(h1b, wk_ref[...]) * (HEAD_DK ** -0.5)
    v = _dot(h1b, wv_ref[...])
    lane = lax.broadcasted_iota(jnp.int32, (1, LANES), 1)
    seq_heads = []
    for s in range(nb):
        rs = slice(s * T, (s + 1) * T)
        gif_s, gif_t, b_c, b_r = gif[rs], gif_ts[s], b_cs[s], b_rs[s]
        m_all = m_s[s]
        m_next = m_all
        heads = []
        for h in range(N_HEADS):
            sl = slice(h * HEAD_DK, (h + 1) * HEAD_DK)
            qh, kh = q[rs, sl], k[rs, sl]
            qb, kb, vb = qh.astype(BF16), kh.astype(BF16), v[rs, sl].astype(BF16)
            bc = b_c[:, N_HEADS + h:N_HEADS + h + 1]
            br = b_r[N_HEADS + h:N_HEADS + h + 1, :]
            ig_c = gif_s[:, h:h + 1]
            ig_r = gif_t[h:h + 1, :]
            m_prev = m_all[:, h:h + 1]
            log_d = jnp.where(causal, bc - br + ig_r, -jnp.inf)
            log_inter = bc + m_prev
            m_t = jnp.maximum(log_inter, jnp.max(log_d, axis=1, keepdims=True))
            d = jnp.exp(log_d - m_t)
            sc = lax.dot_general(qb, kb, (((1,), (1,)), ((), ())), preferred_element_type=F32) * d
            inter = jnp.exp(log_inter - m_t)
            c_h = c_s[s, h]
            n_h = n_s[s, h:h + 1, :]
            num = _dot(sc.astype(BF16), vb) + inter * _dot(qb, c_h.astype(BF16))
            den = jnp.sum(sc, axis=1, keepdims=True) + inter * jnp.sum(qh * n_h, axis=1, keepdims=True)
            heads.append(_ln(num / jnp.maximum(jnp.abs(den), jnp.exp(-m_t))))
            b_last = bc[T - 1:T, :]
            lw_c = b_last - bc + ig_c
            lw_r = b_last - br + ig_r
            m_new = jnp.maximum(b_last + m_prev, jnp.max(lw_r, axis=1, keepdims=True))
            decay = jnp.exp(b_last + m_prev - m_new)
            wk = kh * jnp.exp(lw_c - m_new)
            c_s[s, h] = decay * c_h + _dot(wk.T.astype(BF16), vb)
            n_s[s, h:h + 1, :] = decay * n_h + jnp.sum(wk, axis=0, keepdims=True)
            m_next = jnp.where(lane == h, m_new, m_next)
        m_s[s] = m_next
        seq_heads.append(jnp.concatenate(heads, axis=1))

    xp = xp_ref[...].reshape(R, D_MODEL)
    g1, sh2, sc2 = mod_rows(modp_ref, 2), mod_rows(modp_ref, 3), mod_rows(modp_ref, 4)
    x1 = _ln(DEEPNORM_ALPHA * xp + g1 * mix_s[...]) * ln1g_ref[...] + ln1b_ref[...]
    h2 = _ln(x1) * (1.0 + sc2) + sh2
    x1_ref[...] = x1.reshape(nb, T, D_MODEL)
    h2b = h2.astype(BF16)
    h2_ref[...] = h2b.reshape(nb, T, D_MODEL)
    logits = _dot(h2b, wr_ref[...]) + br_ref[...]
    lt = logits.T[0:N_EXPERTS, :]
    eio = lax.broadcasted_iota(jnp.int32, (N_EXPERTS, R), 0)
    vals, idxs = [], []
    for _ in range(TOP_K):
        mx = jnp.max(lt, axis=0, keepdims=True)
        idx = jnp.min(jnp.where(lt == mx, eio, N_EXPERTS), axis=0, keepdims=True)
        vals.append(mx)
        idxs.append(idx)
        lt = jnp.where(eio == idx, -jnp.inf, lt)
    ex = [jnp.exp(vk - vals[0]) for vk in vals]
    tot = ex[0] + ex[1] + ex[2] + ex[3]
    eidx_ref[0] = jnp.concatenate(idxs, axis=0)
    gate_ref[0] = jnp.concatenate([e / tot for e in ex], axis=0)

    hm = jnp.concatenate(seq_heads, axis=0) * mhg_ref[...]
    hm = hm * _sigmoid(_dot(h1b, wo_ref[...]))
    merged = (_sigmoid(_dot(h1b, wga_ref[...])) * _dot(a.astype(BF16), wpa_ref[...])
              + _sigmoid(_dot(h1b, wgb_ref[...])) * _dot(hm.astype(BF16), wpb_ref[...]))
    mix_s[...] = _dot(merged.astype(BF16), wout_ref[...])

    @pl.when(jnp.logical_and(j == nj - 1, i < n_steps))
    def _():
        hist_ref[...] = pbuf[:, 1:HIST_ROWS, :]
        c_ref[...] = c_s[...]
        n_ref[...] = n_s[...]
        m_ref[...] = m_s[...]


def _mixer(x, mod, hist0, c0, n0, m0, wts, *, T, pos0):
    B, L, D = x.shape
    nj = L // T
    nb = ROW_TILE // T if nj == 1 else 1
    steps = (B // nb) * nj
    kern = functools.partial(_mixer_kernel, T=T, pos0=pos0, nj=nj, n_steps=steps)
    cur = lambda i: jnp.minimum(i, steps - 1)
    prev = lambda i: jnp.maximum(i - 1, 0)
    seq_map = lambda i: (cur(i) // nj, cur(i) % nj, 0)
    seqp_map = lambda i: (prev(i) // nj, prev(i) % nj, 0)
    st3 = lambda i: (cur(i) // nj, 0, 0)
    stp3 = lambda i: (prev(i) // nj, 0, 0)
    st4 = lambda i: (cur(i) // nj, 0, 0, 0)
    step3 = lambda i: (prev(i), 0, 0)
    in_specs = [
        pl.BlockSpec((nb, T, D), seq_map),
        pl.BlockSpec((nb, T, D), seqp_map),
        pl.BlockSpec((nb, 6, D), st3),
        pl.BlockSpec((nb, 6, D), stp3),
        pl.BlockSpec((nb, POOL_HIST, D_POOL), st3),
        pl.BlockSpec((nb, N_HEADS, HEAD_DK, HEAD_DV), st4, pipeline_mode=pl.Buffered(1 if nb > 1 else 2)),
        pl.BlockSpec((nb, N_HEADS, HEAD_DK), st3),
        pl.BlockSpec((nb, 1, LANES), st3),
    ] + [_const_spec(w.shape) for w in wts]
    out_specs = [
        pl.BlockSpec((nb, T, D), seqp_map),
        pl.BlockSpec((nb, T, D), seqp_map),
        pl.BlockSpec((1, TOP_K, nb * T), step3),
        pl.BlockSpec((1, TOP_K, nb * T), step3),
        pl.BlockSpec((nb, POOL_HIST, D_POOL), st3),
        pl.BlockSpec((nb, N_HEADS, HEAD_DK, HEAD_DV), st4),
        pl.BlockSpec((nb, N_HEADS, HEAD_DK), st3),
        pl.BlockSpec((nb, 1, LANES), st3),
    ]
    out_shape = [
        jax.ShapeDtypeStruct((B, L, D), F32),
        jax.ShapeDtypeStruct((B, L, D), BF16),
        jax.ShapeDtypeStruct((steps, TOP_K, nb * T), jnp.int32),
        jax.ShapeDtypeStruct((steps, TOP_K, nb * T), F32),
        jax.ShapeDtypeStruct((B, POOL_HIST, D_POOL), F32),
        jax.ShapeDtypeStruct((B, N_HEADS, HEAD_DK, HEAD_DV), F32),
        jax.ShapeDtypeStruct((B, N_HEADS, HEAD_DK), F32),
        jax.ShapeDtypeStruct((B, 1, LANES), F32),
    ]
    scratch = [pltpu.VMEM((nb, T + HIST_ROWS, D_POOL), F32),
               pltpu.VMEM((nb, N_HEADS, HEAD_DK, HEAD_DV), F32),
               pltpu.VMEM((nb, N_HEADS, HEAD_DK), F32),
               pltpu.VMEM((nb, 1, LANES), F32),
               pltpu.VMEM((nb * T, D), F32)]
    return pl.pallas_call(
        kern,
        grid=(steps + 1,),
        in_specs=in_specs,
        out_specs=out_specs,
        out_shape=out_shape,
        scratch_shapes=scratch,
        compiler_params=pltpu.CompilerParams(
            dimension_semantics=("arbitrary",), vmem_limit_bytes=VMEM_LIMIT),
        name="mixer",
    )(x, x, mod, mod, hist0, c0, n0, m0, *wts)


def _pad8(x):
    return jnp.floor((x + (SUBLANES - 1)) * (1.0 / SUBLANES)) * SUBLANES


def _expert_counts_row(oh, T):
    ones = jnp.ones((SUBLANES, T), BF16)
    return lax.dot_general(ones, oh.astype(BF16), (((1,), (1,)), ((), ())), preferred_element_type=F32)[0:1]


def _local_positions(e, T):
    eio = lax.broadcasted_iota(jnp.int32, (N_EXPERTS, T), 0)
    hits = [eio == e[k:k + 1, :] for k in range(TOP_K)]
    oh = sum(h.astype(F32) for h in hits)
    row = lax.broadcasted_iota(jnp.int32, (T, T), 0)
    col = lax.broadcasted_iota(jnp.int32, (T, T), 1)
    before = _dot(oh.astype(BF16), (row < col).astype(BF16))
    len8 = _pad8(_expert_counts_row(oh, T))
    er = lax.broadcasted_iota(jnp.int32, (N_EXPERTS, N_EXPERTS), 0)
    ec = lax.broadcasted_iota(jnp.int32, (N_EXPERTS, N_EXPERTS), 1)
    start = jnp.sum(jnp.where(ec < er, len8, 0.0), axis=1, keepdims=True)
    base = before + start
    return [jnp.sum(jnp.where(h, base, 0.0), axis=0, keepdims=True) for h in hits]


def _count_kernel(e_ref, cnt_ref, *, T, tiles):
    eio = lax.broadcasted_iota(jnp.int32, (LANES, T), 0)
    for b in range(tiles):
        e = e_ref[:, b * T:(b + 1) * T]
        oh = sum((eio == e[k:k + 1, :]).astype(F32) for k in range(TOP_K))
        cnt_ref[b] = _expert_counts_row(oh, T).astype(jnp.int32)


def _tile_counts(eidx, *, T):
    K, N = eidx.shape
    n_tiles = N // T
    tiles = next(t for t in (8, 4, 2, 1) if n_tiles % t == 0)
    return pl.pallas_call(
        functools.partial(_count_kernel, T=T, tiles=tiles),
        grid=(n_tiles // tiles,),
        in_specs=[pl.BlockSpec((K, tiles * T), lambda i: (0, i))],
        out_specs=pl.BlockSpec((tiles, 1, LANES), lambda i: (i, 0, 0)),
        out_shape=jax.ShapeDtypeStruct((n_tiles, 1, LANES), jnp.int32),
        name="tile_counts",
    )(eidx)


def _start_piece_copies(rows_ref, make_copy):
    for p in range(SORT_ROWS // SUBLANES):
        make_copy(p * SUBLANES, pl.multiple_of(rows_ref[0, 0, p], SUBLANES)).start()


def _dispatch_kernel(fill_ref, rows_ref, e_ref, ha_ref, hb_ref, xs_ref, sbuf, zbuf, sem, zsem, *, T, tiles_a):
    i = pl.program_id(0)
    n_tiles = pl.num_programs(0) - 1
    slot = i % 2

    def zero_fill(finish):
        def go(off, size):
            cp = pltpu.make_async_copy(zbuf.at[pl.ds(0, size)], xs_ref.at[pl.ds(off, size)], zsem)
            cp.wait() if finish else cp.start()

        for e in range(N_EXPERTS):
            off, pad = pl.multiple_of(fill_ref[e], SUBLANES), fill_ref[N_EXPERTS + e]
            for bit in range(SUBLANES.bit_length() - 1, MOE_ROWS.bit_length() - 1):
                size = 1 << bit
                pl.when((pad & size) != 0)(functools.partial(go, off, size))
                off = pl.multiple_of(off + (pad & size), SUBLANES)

        def tail(blk, c):
            go(pl.multiple_of(blk * MOE_ROWS, MOE_ROWS), MOE_ROWS)
            return c
        lax.fori_loop(fill_ref[2 * N_EXPERTS], xs_ref.shape[0] // MOE_ROWS, tail, 0)

    @pl.when(i == 0)
    def _():
        zbuf[...] = jnp.zeros_like(zbuf)
        zero_fill(False)
        zero_fill(True)
        sbuf[1] = jnp.zeros(sbuf.shape[1:], F32)

    def all_pieces(s):
        return pltpu.make_async_copy(sbuf.at[s], xs_ref.at[pl.ds(0, SORT_ROWS)], sem.at[s])

    @pl.when(i >= 1)
    def _():
        all_pieces(slot).wait()

    _start_piece_copies(rows_ref, lambda r, g: pltpu.make_async_copy(
        sbuf.at[1 - slot, pl.ds(r, SUBLANES)], xs_ref.at[pl.ds(g, SUBLANES)], sem.at[1 - slot]))

    pos = [p.astype(jnp.int32) for p in _local_positions(e_ref[...], T)]
    rio = lax.broadcasted_iota(jnp.int32, (SORT_ROWS, T), 0)
    pick = (rio == pos[0]) | (rio == pos[1]) | (rio == pos[2]) | (rio == pos[3])
    h = jnp.where(i < tiles_a, ha_ref[...], hb_ref[...])
    sbuf[slot] = _dot(pick.astype(BF16), h)

    @pl.when(i == n_tiles)
    def _():
        all_pieces(1 - slot).wait()


def _dispatch(fill, piece_rows, eidx, h2_a, h2_b, n_rows, *, T):
    D = h2_a.shape[1]
    tiles_a, tiles_b = h2_a.shape[0] // T, h2_b.shape[0] // T
    n_tiles = tiles_a + tiles_b
    grid_spec = pltpu.PrefetchScalarGridSpec(
        num_scalar_prefetch=1,
        grid=(n_tiles + 1,),
        in_specs=[pl.BlockSpec((1, 1, PIECE_LANES), lambda i, fill: (i, 0, 0), memory_space=pltpu.SMEM),
                  pl.BlockSpec((TOP_K, T), lambda i, fill: (0, jnp.minimum(i, n_tiles - 1))),
                  pl.BlockSpec((T, D), lambda i, fill: (jnp.minimum(i, tiles_a - 1), 0)),
                  pl.BlockSpec((T, D), lambda i, fill: (jnp.clip(i - tiles_a, 0, tiles_b - 1), 0))],
        out_specs=pl.BlockSpec(memory_space=pl.ANY),
        scratch_shapes=[pltpu.VMEM((2, SORT_ROWS, D), F32), pltpu.VMEM((MOE_ROWS, D), F32),
                        pltpu.SemaphoreType.DMA((2,)), pltpu.SemaphoreType.DMA(())],
    )
    return pl.pallas_call(
        functools.partial(_dispatch_kernel, T=T, tiles_a=tiles_a),
        grid_spec=grid_spec,
        out_shape=jax.ShapeDtypeStruct((n_rows, D), F32),
        compiler_params=pltpu.CompilerParams(
            dimension_semantics=("arbitrary",), vmem_limit_bytes=VMEM_LIMIT),
        name="dispatch",
    )(fill, piece_rows, eidx, h2_a, h2_b)


def _moe_kernel(be_ref, nu_ref, xs_ref, wgu_ref, bgu_ref, wd_ref, bd_ref, y_ref, wgu_s, wd_s):
    i = pl.program_id(0)
    used = i < nu_ref[0]
    new_expert = jnp.logical_or(i == 0, be_ref[i] != be_ref[jnp.maximum(i - 1, 0)])

    @pl.when(jnp.logical_and(used, new_expert))
    def _():
        def cast_rows(r, c):
            rows = pl.ds(pl.multiple_of(r * LANES, LANES), LANES)
            wgu_s[rows, :] = wgu_ref[0, rows, :].astype(BF16)
            wd_s[rows, :] = wd_ref[0, rows, :].astype(BF16)
            return c
        lax.fori_loop(0, D_FF // LANES, cast_rows, 0)

    @pl.when(used)
    def _():
        x = xs_ref[...].astype(BF16)
        bgu = bgu_ref[0]
        acc = None
        for c in range(D_FF // FF_CHUNK):
            cs = slice(c * FF_CHUNK, (c + 1) * FF_CHUNK)
            ls = slice(D_FF + c * FF_CHUNK, D_FF + (c + 1) * FF_CHUNK)
            glu = jnp.minimum(_dot(x, wgu_s[:, cs]) + bgu[:, cs], SWIGLU_LIMIT)
            lin = jnp.clip(_dot(x, wgu_s[:, ls]) + bgu[:, ls], -SWIGLU_LIMIT, SWIGLU_LIMIT)
            act = glu * _sigmoid(SWIGLU_ALPHA * glu) * (lin + 1.0)
            part = _dot(act.astype(BF16), wd_s[cs, :])
            acc = part if acc is None else acc + part
        y_ref[...] = acc + bd_ref[0]

    @pl.when(jnp.logical_not(used))
    def _():
        def zero_rows(r, c):
            rows = pl.ds(pl.multiple_of(r * LANES, LANES), LANES)
            y_ref[rows, :] = jnp.zeros((LANES, y_ref.shape[1]), F32)
            return c
        lax.fori_loop(0, y_ref.shape[0] // LANES, zero_rows, 0)


def _moe(block_e, n_used, xs, w_gu, b_gu, w_down, b_down):
    P, D = xs.shape
    R = MOE_ROWS
    nb = P // R
    xs_map = lambda i, be, nu: (jnp.minimum(i, nu[0] - 1), 0)
    w_map = lambda i, be, nu: (be[i], 0, 0)
    grid_spec = pltpu.PrefetchScalarGridSpec(
        num_scalar_prefetch=2,
        grid=(nb,),
        in_specs=[
            pl.BlockSpec((R, D), xs_map),
            pl.BlockSpec((1, D, 2 * D_FF), w_map),
            pl.BlockSpec((1, 1, 2 * D_FF), w_map),
            pl.BlockSpec((1, D_FF, D), w_map),
            pl.BlockSpec((1, 1, D), w_map),
        ],
        out_specs=pl.BlockSpec((R, D), lambda i, be, nu: (i, 0)),
        scratch_shapes=[pltpu.VMEM((D, 2 * D_FF), BF16), pltpu.VMEM((D_FF, D), BF16)],
    )
    return pl.pallas_call(
        _moe_kernel,
        grid_spec=grid_spec,
        out_shape=jax.ShapeDtypeStruct((P, D), F32),
        compiler_params=pltpu.CompilerParams(
            dimension_semantics=("arbitrary",), vmem_limit_bytes=VMEM_LIMIT),
        name="moe_experts",
    )(block_e, n_used, xs, w_gu, b_gu, w_down, b_down)


def _combine_kernel(rows_ref, rowsn_ref, e_ref, gate_ref, x1_ref, mod_ref, g_ref, b_ref, y_hbm,
                    x2_ref, ybuf, sem, *, T):
    nj = pl.num_programs(1)
    step = pl.program_id(0) * nj + pl.program_id(1)
    n_steps = pl.num_programs(0) * nj
    slot = step % 2
    S = SORT_ROWS

    def piece_copy(s):
        return lambda r, g: pltpu.make_async_copy(
            y_hbm.at[pl.ds(g, SUBLANES)], ybuf.at[s, pl.ds(r, SUBLANES)], sem.at[s])

    def all_pieces(s):
        return pltpu.make_async_copy(y_hbm.at[pl.ds(0, S)], ybuf.at[s], sem.at[s])

    @pl.when(step == 0)
    def _():
        _start_piece_copies(rows_ref, piece_copy(0))

    all_pieces(slot).wait()
    _start_piece_copies(rowsn_ref, piece_copy(1 - slot))

    pos = _local_positions(e_ref[...], T)
    stacked = jnp.concatenate(pos + [gate_ref[...], jnp.zeros((LANES - 2 * TOP_K, T), F32)], axis=0).T
    pos_c = [stacked[:, k:k + 1].astype(jnp.int32) for k in range(TOP_K)]
    gate_c = [stacked[:, TOP_K + k:TOP_K + k + 1] for k in range(TOP_K)]
    cio = lax.broadcasted_iota(jnp.int32, (T, S), 1)
    w = jnp.where(cio == pos_c[0], gate_c[0], 0.0)
    for k in range(1, TOP_K):
        w = w + jnp.where(cio == pos_c[k], gate_c[k], 0.0)
    ffn = _dot(w.astype(BF16), ybuf[slot].astype(BF16))
    bt, lt, d = x1_ref.shape
    g2 = mod_ref[...][:, 5:6, :]
    z = DEEPNORM_ALPHA * x1_ref[...] + g2 * ffn.reshape(bt, lt, d)
    x2_ref[...] = _ln(z) * g_ref[...] + b_ref[...]

    @pl.when(step == n_steps - 1)
    def _():
        all_pieces(1 - slot).wait()


def _combine(piece_rows, tile0, eidx, gates, x1, mod, ln_g, ln_b, y, *, T):
    B, L, D = x1.shape
    bt, lt = (1, T) if L >= T else (T // L, L)
    nj = L // lt
    n_tiles = (B // bt) * nj
    cur = lambda b, j: (0, b * nj + j)
    tab_cur = lambda b, j: (tile0 + b * nj + j, 0, 0)
    tab_nxt = lambda b, j: (tile0 + jnp.minimum(b * nj + j + 1, n_tiles - 1), 0, 0)
    return pl.pallas_call(
        functools.partial(_combine_kernel, T=T),
        grid=(B // bt, nj),
        in_specs=[
            pl.BlockSpec((1, 1, PIECE_LANES), tab_cur, memory_space=pltpu.SMEM),
            pl.BlockSpec((1, 1, PIECE_LANES), tab_nxt, memory_space=pltpu.SMEM),
            pl.BlockSpec((TOP_K, T), cur),
            pl.BlockSpec((TOP_K, T), cur),
            pl.BlockSpec((bt, lt, D), lambda b, j: (b, j, 0)),
            pl.BlockSpec((bt, 6, D), lambda b, j: (b, 0, 0)),
            pl.BlockSpec((1, D), lambda b, j: (0, 0)),
            pl.BlockSpec((1, D), lambda b, j: (0, 0)),
            pl.BlockSpec(memory_space=pl.ANY),
        ],
        out_specs=pl.BlockSpec((bt, lt, D), lambda b, j: (b, j, 0)),
        out_shape=jax.ShapeDtypeStruct((B, L, D), F32),
        scratch_shapes=[pltpu.VMEM((2, SORT_ROWS, D), F32), pltpu.SemaphoreType.DMA((2,))],
        compiler_params=pltpu.CompilerParams(
            dimension_semantics=("arbitrary", "arbitrary"), vmem_limit_bytes=VMEM_LIMIT),
        name="combine",
    )(piece_rows, piece_rows, eidx, gates, x1, mod, ln_g, ln_b, y)


def _routing_tables(cnt, n_blocks):
    R = MOE_ROWS
    n_tiles = cnt.shape[0]
    len8 = (cnt + SUBLANES - 1) // SUBLANES * SUBLANES
    total = jnp.sum(len8, axis=0)
    padded = (total + R - 1) // R * R
    pad_end = jnp.cumsum(padded)
    pad_start = pad_end - padded
    first_row = pad_start[None, :] + jnp.cumsum(len8, axis=0) - len8
    tile_end = jnp.cumsum(len8, axis=1)
    piece0 = jnp.arange(SORT_ROWS // SUBLANES, dtype=jnp.int32) * SUBLANES
    in_run = ((piece0[None, :, None] >= (tile_end - len8)[:, None, :])
              & (piece0[None, :, None] < tile_end[:, None, :]))
    shift = jnp.sum(jnp.where(in_run, (first_row - (tile_end - len8))[:, None, :], 0), axis=-1)
    spare = n_blocks * R + piece0
    rows = jnp.where(jnp.any(in_run, axis=-1), piece0[None, :] + shift, spare[None, :])
    rows = jnp.concatenate([spare[None, :], rows], axis=0)
    rows = jnp.pad(rows, ((0, 0), (0, PIECE_LANES - rows.shape[1])))
    piece_rows = rows.astype(jnp.int32).reshape(n_tiles + 1, 1, PIECE_LANES)
    blk0 = jnp.arange(n_blocks + -(-SORT_ROWS // R), dtype=jnp.int32) * R
    block_e = jnp.sum((blk0[:, None] >= pad_end[None, :]).astype(jnp.int32), axis=1)
    block_e = jnp.minimum(block_e, N_EXPERTS - 1)
    n_used = (pad_end[-1:] // R).astype(jnp.int32)
    fill = jnp.concatenate([pad_start + total, padded - total, n_used]).astype(jnp.int32)
    return piece_rows, block_e, n_used, fill


def _mix_group(x, mod, hist0, c0, n0, m0, mixer_w, *, T, pos0):
    B, L, _ = x.shape
    N = B * L
    m0p = jnp.pad(m0, ((0, 0), (0, LANES - N_HEADS))).reshape(B, 1, LANES)
    x1, h2, eidx, gates, hist, c_new, n_new, m_new = _mixer(
        x, mod, hist0, c0, n0, m0p, mixer_w, T=T, pos0=pos0)
    eidx = eidx.transpose(1, 0, 2).reshape(TOP_K, N)
    gates = gates.transpose(1, 0, 2).reshape(TOP_K, N)
    states = (hist, c_new, n_new, m_new.reshape(B, LANES)[:, :N_HEADS])
    return x1, h2.reshape(N, -1), eidx, gates, states


def kernel(x_prompt, x_sample, state_pool, state_mlstm_C, state_mlstm_n, state_mlstm_m, c_prompt, c_sample,
           w_ada, b_ada, w_in, b_if, w_group, pool_scale, mh_norm_g, w_proj_a, w_proj_b, w_out,
           ln1_g, ln1_b, w_router, b_router, w_gu, b_gu, w_down, b_down, ln2_g, ln2_b):
    assert w_ada.shape[0] == DEPTH == 1
    l = 0
    bp, bs = x_prompt.shape[0], x_sample.shape[0]
    dt = x_prompt.dtype
    row = lambda a: a.reshape(1, -1)

    wi = w_in[l]
    o0 = 0
    parts = []
    for width in (D_POOL, N_HEADS * HEAD_DK, N_HEADS * HEAD_DK, N_HEADS * HEAD_DV, N_HEADS * HEAD_DV):
        parts.append(wi[:, o0:o0 + width].astype(BF16))
        o0 += width
    w_if = jnp.pad(wi[:, o0:o0 + 2 * N_HEADS], ((0, 0), (0, LANES - 2 * N_HEADS))).astype(BF16)
    o0 += 2 * N_HEADS
    w_ga = wi[:, o0:o0 + D_MODEL].astype(BF16)
    w_gb = wi[:, o0 + D_MODEL:o0 + 2 * D_MODEL].astype(BF16)
    w_u, w_q, w_k, w_v, w_o = parts
    b_if_row = jnp.pad(b_if[l], (0, LANES - 2 * N_HEADS)).reshape(1, LANES)
    w_r = jnp.pad(w_router[l], ((0, 0), (0, LANES - N_EXPERTS))).astype(BF16)
    b_r = jnp.pad(b_router[l], (0, LANES - N_EXPERTS), constant_values=NEG_BIG).reshape(1, LANES)
    mixer_w = (w_u, w_q, w_k, w_v, w_o, w_ga, w_gb, w_if, b_if_row,
               w_group[l].astype(BF16), row(pool_scale[l]), row(mh_norm_g[l]),
               w_proj_a[l].astype(BF16), w_proj_b[l].astype(BF16), w_out[l].astype(BF16),
               row(ln1_g[l]), row(ln1_b[l]), w_r, b_r)
    moe_w = (w_gu[l], b_gu[l].reshape(N_EXPERTS, 1, 2 * D_FF),
             w_down[l], b_down[l].reshape(N_EXPERTS, 1, D_MODEL))

    mod = _ada(jnp.concatenate([c_prompt, c_sample], axis=0), w_ada[l], b_ada[l])
    mod = mod.reshape(bp + bs, 6, D_MODEL)

    zeros = lambda *s: jnp.zeros(s, dt)
    x1_p, h2_p, eidx_p, gates_p, st_p = _mix_group(
        x_prompt, mod[:bp], zeros(bp, POOL_HIST, D_POOL), zeros(bp, N_HEADS, HEAD_DK, HEAD_DV),
        zeros(bp, N_HEADS, HEAD_DK), zeros(bp, N_HEADS), mixer_w, T=min(256, x_prompt.shape[1]), pos0=0)
    x1_s, h2_s, eidx_s, gates_s, st_s = _mix_group(
        x_sample, mod[bp:], state_pool[l], state_mlstm_C[l], state_mlstm_n[l], state_mlstm_m[l],
        mixer_w, T=min(256, x_sample.shape[1]), pos0=PAST_LEN)

    n_p, n_s = h2_p.shape[0], h2_s.shape[0]
    eidx = jnp.concatenate([eidx_p, eidx_s], axis=1)
    T = ROW_TILE
    n_tiles = (n_p + n_s) // T
    cnt = _tile_counts(eidx, T=T)[:, 0, :N_EXPERTS]
    max_rows = (n_p + n_s) * TOP_K + n_tiles * N_EXPERTS * (SUBLANES - 1) + N_EXPERTS * (MOE_ROWS - 1)
    n_blocks = -(-max_rows // MOE_ROWS)
    piece_rows, block_e, n_used, fill = _routing_tables(cnt, n_blocks)
    n_rows = (n_blocks + -(-SORT_ROWS // MOE_ROWS)) * MOE_ROWS
    xs = _dispatch(fill, piece_rows, eidx, h2_p, h2_s, n_rows, T=T)
    y = _moe(block_e, n_used, xs, *moe_w)
    ln2 = (row(ln2_g[l]), row(ln2_b[l]))
    yp = _combine(piece_rows, 1, eidx_p, gates_p, x1_p, mod[:bp], *ln2, y, T=T)
    ys = _combine(piece_rows, 1 + n_p // T, eidx_s, gates_s, x1_s, mod[bp:], *ln2, y, T=T)
    st = lambda a: a[None]
    return (yp, ys) + tuple(st(a) for a in st_p) + tuple(st(a) for a in st_s)
```
